```python
import math
import jax
import jax.numpy as jnp
from jax import lax
import numpy as np

D_MODEL = 1024
BATCH = 8
SEQ = 4096
DEPTH = 4

N_MIXERS = 2
N_A_LAYERS = (DEPTH + N_MIXERS - 1) // N_MIXERS
N_B_LAYERS = DEPTH // N_MIXERS
RMS_EPS = 1e-6
D_FF = 2816

DIL_PAIRS = ((128, 1), (512, 4), (2048, 16))
N_GROUPS = len(DIL_PAIRS)
HEADS_PER_GROUP = 4
HEAD_DIM_A = 128
ATTN_WIDTH = N_GROUPS * HEADS_PER_GROUP * HEAD_DIM_A
ROPE_THETA = 10000.0

GDN_K_HEADS = 8
GDN_V_HEADS = 16
GDN_HEAD_DIM = 128
GDN_KD = GDN_K_HEADS * GDN_HEAD_DIM
GDN_VD = GDN_V_HEADS * GDN_HEAD_DIM
GDN_CONV_DIM = 2 * GDN_KD + GDN_VD
GDN_PROJ = GDN_CONV_DIM + GDN_VD + 2 * GDN_V_HEADS
GDN_CONV = 4
GDN_CHUNK = 64

kernel_name = "hybrid_dilated_attn_gated_deltanet_macaron"


def rms_norm(x, w):
    xf = x.astype(jnp.float32)
    y = xf * lax.rsqrt(jnp.mean(xf * xf, axis=-1, keepdims=True) + RMS_EPS)
    return (y * w.astype(jnp.float32)).astype(x.dtype)


def swiglu(h, w_in, w_out):
    gu = h @ w_in
    gate, up = gu[..., :D_FF], gu[..., D_FF:]
    return (jax.nn.silu(gate) * up) @ w_out


def rope_tables(seq, dim):
    inv_freq = 1.0 / (ROPE_THETA ** (jnp.arange(0, dim, 2, dtype=jnp.float32) / dim))
    ang = jnp.arange(seq, dtype=jnp.float32)[:, None] * inv_freq[None, :]
    return jnp.cos(ang), jnp.sin(ang)


def apply_rope(x, cos, sin):
    xf = x.astype(jnp.float32)
    half = xf.shape[-1] // 2
    x1, x2 = xf[..., :half], xf[..., half:]
    c, s = cos[None, :, None, :], sin[None, :, None, :]
    return jnp.concatenate([x1 * c - x2 * s, x2 * c + x1 * s], axis=-1).astype(x.dtype)


def dilated_band_attention(q, k, v, dilation, span):
    B, S, H, E = q.shape
    Ls = S // dilation
    nb = -(-Ls // span)
    Lp = nb * span

    def fold(t):
        t = t.reshape(B, Ls, dilation, H, E).transpose(0, 2, 3, 1, 4)
        t = jnp.pad(t, ((0, 0), (0, 0), (0, 0), (0, Lp - Ls), (0, 0)))
        return t.reshape(B, dilation, H, nb, span, E)

    def with_prev(t):
        prev = jnp.pad(t[:, :, :, :-1], ((0, 0), (0, 0), (0, 0), (1, 0), (0, 0), (0, 0)))
        return jnp.concatenate([prev, t], axis=4)

    qb = fold(q)
    kw = with_prev(fold(k))
    vw = with_prev(fold(v))
    s = jnp.einsum('bdhnqe,bdhnke->bdhnqk', qb, kw).astype(jnp.float32)
    blk = jnp.arange(nb)[:, None, None]
    qi = jnp.arange(span)[None, :, None] + span
    kj = jnp.arange(2 * span)[None, None, :]
    dist = qi - kj
    mask = (dist >= 0) & (dist <= span) & (blk * span - span + kj >= 0)
    s = jnp.where(mask, s, -jnp.inf)
    m = jnp.max(s, axis=-1, keepdims=True)
    p = jnp.exp(s - m)
    den = jnp.sum(p, axis=-1, keepdims=True)
    o = jnp.einsum('bdhnqk,bdhnke->bdhnqe', (p / den).astype(v.dtype), vw)
    lse = (m + jnp.log(den))[..., 0]
    o = o.reshape(B, dilation, H, Lp, E)[:, :, :, :Ls].transpose(0, 3, 1, 2, 4).reshape(B, S, H, E)
    lse = lse.reshape(B, dilation, H, Lp)[:, :, :, :Ls].transpose(0, 3, 1, 2).reshape(B, S, H)
    return o, lse


def dilated_attention(h, w_in, w_out, cos, sin):
    B, S, _ = h.shape
    qkv = (h @ w_in).reshape(B, S, 3, N_GROUPS * HEADS_PER_GROUP, HEAD_DIM_A)
    q = apply_rope(qkv[:, :, 0], cos, sin) * (HEAD_DIM_A ** -0.5)
    k = apply_rope(qkv[:, :, 1], cos, sin)
    v = qkv[:, :, 2]
    gshape = (B, S, N_GROUPS, HEADS_PER_GROUP, HEAD_DIM_A)
    q, k, v = q.reshape(gshape), k.reshape(gshape), v.reshape(gshape)
    outs, lses = [], []
    for g, (window, dil) in enumerate(DIL_PAIRS):
        o, l = dilated_band_attention(q[:, :, g], k[:, :, g], v[:, :, g], dil, window // dil)
        outs.append(o)
        lses.append(l)
    alpha = jax.nn.softmax(jnp.stack(lses, axis=2), axis=2)
    o = jnp.stack(outs, axis=2) * alpha[..., None].astype(v.dtype)
    return o.reshape(B, S, ATTN_WIDTH) @ w_out


def causal_depthwise_conv(x, w):
    K = w.shape[0]
    S = x.shape[1]
    xp = jnp.pad(x, ((0, 0), (K - 1, 0), (0, 0)))
    y = xp[:, 0:S] * w[0]
    for j in range(1, K):
        y = y + xp[:, j:j + S] * w[j]
    return y


def l2norm(x):
    xf = x.astype(jnp.float32)
    return xf * lax.rsqrt(jnp.sum(xf * xf, axis=-1, keepdims=True) + 1e-6)


def chunk_gated_delta_rule(q, k, v, g, beta):
    B, S, H, DK = q.shape
    DV = v.shape[-1]
    C = GDN_CHUNK
    nc = S // C

    def chunks4(t):
        return t.reshape(B, nc, C, H, t.shape[-1]).transpose(1, 0, 3, 2, 4)

    def chunks3(t):
        return t.reshape(B, nc, C, H).transpose(1, 0, 3, 2)

    qc, kc, vc = chunks4(q), chunks4(k), chunks4(v)
    gc, bc = chunks3(g), chunks3(beta)
    gcum = jnp.cumsum(gc, axis=-1)
    idx = jnp.arange(C)
    incl = idx[:, None] >= idx[None, :]
    strict = idx[:, None] > idx[None, :]
    diff = gcum[..., :, None] - gcum[..., None, :]
    decay = jnp.where(incl, jnp.exp(jnp.where(incl, diff, 0.0)), 0.0)
    kk = jnp.einsum('nbhie,nbhje->nbhij', kc, kc)
    L = jnp.where(strict, bc[..., :, None] * kk * decay, 0.0)
    rhs = jnp.concatenate([vc * bc[..., None], kc * (bc * jnp.exp(gcum))[..., None]], axis=-1)
    sol = lax.linalg.triangular_solve(jnp.eye(C, dtype=jnp.float32) + L, rhs,
                                      left_side=True, lower=True, unit_diagonal=True)
    u, w = sol[..., :DV], sol[..., DV:]
    qk = jnp.einsum('nbhie,nbhje->nbhij', qc, kc) * decay
    q_dec = qc * jnp.exp(gcum)[..., None]
    k_dec = kc * jnp.exp(gcum[..., -1:] - gcum)[..., None]
    c_dec = jnp.exp(gcum[..., -1])

    def step(state, inp):
        u_c, w_c, qk_c, qd_c, kd_c, cd_c = inp
        v_new = u_c - jnp.einsum('bhce,bhef->bhcf', w_c, state)
        o_c = jnp.einsum('bhce,bhef->bhcf', qd_c, state) + jnp.einsum('bhij,bhjf->bhif', qk_c, v_new)
        state = state * cd_c[..., None, None] + jnp.einsum('bhce,bhcf->bhef', kd_c, v_new)
        return state, o_c

    s0 = jnp.zeros((B, H, DK, DV), jnp.float32)
    _, o = lax.scan(step, s0, (u, w, qk, q_dec, k_dec, c_dec))
    return o.transpose(1, 0, 3, 2, 4).reshape(B, S, H, DV)


def gated_deltanet(h, w_in, conv_w, a_log, dt_bias, norm_w, w_out):
    B, S, _ = h.shape
    E = GDN_HEAD_DIM
    proj = h @ w_in
    qkv = jax.nn.silu(causal_depthwise_conv(proj[..., :GDN_CONV_DIM], conv_w))
    z = proj[..., GDN_CONV_DIM:GDN_CONV_DIM + GDN_VD]
    b = proj[..., GDN_CONV_DIM + GDN_VD:GDN_CONV_DIM + GDN_VD + GDN_V_HEADS]
    a = proj[..., GDN_CONV_DIM + GDN_VD + GDN_V_HEADS:]
    q = qkv[..., :GDN_KD].reshape(B, S, GDN_K_HEADS, E)
    k = qkv[..., GDN_KD:2 * GDN_KD].reshape(B, S, GDN_K_HEADS, E)
    v = qkv[..., 2 * GDN_KD:].reshape(B, S, GDN_V_HEADS, E).astype(jnp.float32)
    rep = GDN_V_HEADS // GDN_K_HEADS
    q = jnp.repeat(l2norm(q), rep, axis=2) * (E ** -0.5)
    k = jnp.repeat(l2norm(k), rep, axis=2)
    beta = jax.nn.sigmoid(b.astype(jnp.float32))
    g = -jnp.exp(a_log.astype(jnp.float32)) * jax.nn.softplus(a.astype(jnp.float32) + dt_bias.astype(jnp.float32))
    o = chunk_gated_delta_rule(q, k, v, g, beta)
    o = rms_norm(o, norm_w) * jax.nn.silu(z.reshape(B, S, GDN_V_HEADS, E).astype(jnp.float32))
    return o.astype(h.dtype).reshape(B, S, GDN_VD) @ w_out


def setup_inputs(seed: int = 0) -> dict:
    key = jax.random.key(seed)
    ks = jax.random.split(key, 16)
    f32 = jnp.float32
    x = jax.random.normal(ks[0], (BATCH, SEQ, D_MODEL), f32)
    norm_w = 1.0 + 0.02 * jax.random.normal(ks[1], (DEPTH, 3, D_MODEL), f32)
    ffn_w_in = jax.random.normal(ks[2], (DEPTH, 2, D_MODEL, 2 * D_FF), f32) * D_MODEL ** -0.5
    ffn_w_out = jax.random.normal(ks[3], (DEPTH, 2, D_FF, D_MODEL), f32) * D_FF ** -0.5
    attn_w_in = jax.random.normal(ks[4], (N_A_LAYERS, D_MODEL, 3 * ATTN_WIDTH), f32) * D_MODEL ** -0.5
    attn_w_out = jax.random.normal(ks[5], (N_A_LAYERS, ATTN_WIDTH, D_MODEL), f32) * ATTN_WIDTH ** -0.5
    gdn_w_in = jax.random.normal(ks[6], (N_B_LAYERS, D_MODEL, GDN_PROJ), f32) * D_MODEL ** -0.5
    gdn_conv_w = jax.random.normal(ks[7], (N_B_LAYERS, GDN_CONV, GDN_CONV_DIM), f32) * GDN_CONV ** -0.5
    gdn_a_log = jnp.log(jax.random.uniform(ks[8], (N_B_LAYERS, GDN_V_HEADS), f32, 1.0, 16.0))
    dt = jnp.exp(jax.random.uniform(ks[9], (N_B_LAYERS, GDN_V_HEADS), f32, math.log(1e-3), math.log(1e-1)))
    gdn_dt_bias = dt + jnp.log(-jnp.expm1(-dt))
    gdn_norm_w = 1.0 + 0.02 * jax.random.normal(ks[10], (N_B_LAYERS, GDN_HEAD_DIM), f32)
    gdn_w_out = jax.random.normal(ks[11], (N_B_LAYERS, GDN_VD, D_MODEL), f32) * GDN_VD ** -0.5
    final_norm_w = 1.0 + 0.02 * jax.random.normal(ks[12], (D_MODEL,), f32)
    return {"x": x, "norm_w": norm_w, "ffn_w_in": ffn_w_in, "ffn_w_out": ffn_w_out,
            "attn_w_in": attn_w_in, "attn_w_out": attn_w_out,
            "gdn_w_in": gdn_w_in, "gdn_conv_w": gdn_conv_w, "gdn_a_log": gdn_a_log,
            "gdn_dt_bias": gdn_dt_bias, "gdn_norm_w": gdn_norm_w, "gdn_w_out": gdn_w_out,
            "final_norm_w": final_norm_w}


def reference(x, norm_w, ffn_w_in, ffn_w_out, attn_w_in, attn_w_out, gdn_w_in, gdn_conv_w,
              gdn_a_log, gdn_dt_bias, gdn_norm_w, gdn_w_out, final_norm_w):
    S = x.shape[1]
    cos, sin = rope_tables(S, HEAD_DIM_A)
    ia, ib = 0, 0
    for i in range(DEPTH):
        x = x + 0.5 * swiglu(rms_norm(x, norm_w[i, 0]), ffn_w_in[i, 0], ffn_w_out[i, 0])
        h = rms_norm(x, norm_w[i, 1])
        if i % N_MIXERS == 0:
            x = x + dilated_attention(h, attn_w_in[ia], attn_w_out[ia], cos, sin)
            ia += 1
        else:
            x = x + gated_deltanet(h, gdn_w_in[ib], gdn_conv_w[ib], gdn_a_log[ib], gdn_dt_bias[ib],
                                   gdn_norm_w[ib], gdn_w_out[ib])
            ib += 1
        x = x + 0.5 * swiglu(rms_norm(x, norm_w[i, 2]), ffn_w_in[i, 1], ffn_w_out[i, 1])
    return rms_norm(x, final_norm_w)
```

```python
import functools

import jax
import jax.numpy as jnp
from jax import lax
from jax.experimental import pallas as pl
from jax.experimental.pallas import tpu as pltpu

F32, BF16 = jnp.float32, jnp.bfloat16

RMS_EPS = 1e-6
L2_EPS = 1e-6
LANES = 128
ROPE_THETA = 10000.0
DIL_PAIRS = ((128, 1), (512, 4), (2048, 16))
HEADS_PER_GROUP = 4
SPAN = 128
GDN_K_HEADS = 8
GDN_V_HEADS = 16
GDN_CHUNK = 64
GDN_CONV = 4
GDN_HEAD_BLOCK = 4
GDN_TIME_BLOCK = 256
TOKEN_TILE = 512
VMEM_LIMIT_BYTES = 56 * 1024 * 1024
NEG_BIG = -1e30


def _cparams(n_axes):
    return pltpu.CompilerParams(dimension_semantics=("arbitrary",) * n_axes,
                                vmem_limit_bytes=VMEM_LIMIT_BYTES)


def _resident(shape):
    nd = len(shape)
    return pl.BlockSpec(shape, lambda *_: (0,) * nd, pipeline_mode=pl.Buffered(1))


def _mm(a, b):
    return jnp.dot(a, b, preferred_element_type=F32)


def _mm_nt(a, b):
    return lax.dot_general(a, b, (((1,), (1,)), ((), ())), preferred_element_type=F32)


def _mm_tn(a, b):
    return lax.dot_general(a, b, (((0,), (0,)), ((), ())), preferred_element_type=F32)


def _rms(x, w):
    ms = jnp.mean(x * x, axis=-1, keepdims=True)
    return x * lax.rsqrt(ms + RMS_EPS) * w


def _silu(x):
    return x * (1.0 / (1.0 + jnp.exp(-x)))


def _softplus(x):
    return jnp.maximum(x, 0.0) + jnp.log1p(jnp.exp(-jnp.abs(x)))


def _ffn_body(x_ref, nw_ref, wg_ref, wu_ref, wo_ref, fw_ref, o_ref, *, final):
    x = x_ref[...]
    h = _rms(x, nw_ref[...]).astype(BF16)
    g = _mm(h, wg_ref[...])
    u = _mm(h, wu_ref[...])
    a = (_silu(g) * u).astype(BF16)
    y = x + 0.5 * _mm(a, wo_ref[...])
    if final:
        y = _rms(y, fw_ref[...])
    o_ref[...] = y


def _ffn(x, nw, wg, wu, wo, fw, final):
    n, d = x.shape
    f = wg.shape[1]
    tm = TOKEN_TILE
    row = pl.BlockSpec((tm, d), lambda i: (i, 0))
    vec = pl.BlockSpec((1, d), lambda i: (0, 0))
    return pl.pallas_call(
        functools.partial(_ffn_body, final=final),
        grid=(n // tm,),
        in_specs=[row, vec, _resident((d, f)), _resident((d, f)), _resident((f, d)), vec],
        out_specs=row,
        out_shape=jax.ShapeDtypeStruct((n, d), F32),
        compiler_params=_cparams(1),
        name="ffn",
    )(x, nw, wg, wu, wo, fw)


def _attn_in_body(x_ref, nw_ref, w_ref, cos_ref, sin_ref, o_ref, *, n_heads, scale):
    h = _rms(x_ref[...], nw_ref[...]).astype(BF16)
    y = _mm(h, w_ref[...])
    cosf = cos_ref[...]
    sinf = sin_ref[...]
    for hd in range(2 * n_heads):
        blk = y[:, hd * LANES:(hd + 1) * LANES]
        r = blk * cosf + pltpu.roll(blk, LANES // 2, 1) * sinf
        if hd < n_heads:
            r = r * scale
        o_ref[:, hd * LANES:(hd + 1) * LANES] = r.astype(BF16)
    o_ref[:, 2 * n_heads * LANES:] = y[:, 2 * n_heads * LANES:].astype(BF16)


def _attn_in(x, nw, w, cosf, sinf, seq):
    n, d = x.shape
    c = w.shape[1]
    tm = TOKEN_TILE
    tiles_per_seq = seq // tm
    return pl.pallas_call(
        functools.partial(_attn_in_body, n_heads=c // (3 * LANES), scale=LANES ** -0.5),
        grid=(n // tm,),
        in_specs=[pl.BlockSpec((tm, d), lambda i: (i, 0)),
                  pl.BlockSpec((1, d), lambda i: (0, 0)),
                  _resident((d, c)),
                  pl.BlockSpec((tm, LANES), lambda i: (i % tiles_per_seq, 0)),
                  pl.BlockSpec((tm, LANES), lambda i: (i % tiles_per_seq, 0))],
        out_specs=pl.BlockSpec((tm, c), lambda i: (i, 0)),
        out_shape=jax.ShapeDtypeStruct((n, c), BF16),
        compiler_params=_cparams(1),
        name="attn_in",
    )(x, nw, w, cosf, sinf)


def _attn_core_body(q_ref, kp_ref, kc_ref, vp_ref, vc_ref, o_ref, lse_ref):
    n = pl.program_id(2)
    qi = lax.broadcasted_iota(jnp.int32, (SPAN, SPAN), 0)
    kj = lax.broadcasted_iota(jnp.int32, (SPAN, SPAN), 1)
    mask_c = kj <= qi
    mask_p = jnp.logical_and(kj >= qi, n > 0)
    lane = lax.broadcasted_iota(jnp.int32, (SPAN, LANES), 1)
    lse_tile = jnp.zeros((SPAN, LANES), F32)
    for hg in range(HEADS_PER_GROUP):
        sl = slice(hg * LANES, (hg + 1) * LANES)
        q = q_ref[0, :, sl]
        sc = jnp.where(mask_c, _mm_nt(q, kc_ref[0, :, sl]), NEG_BIG)
        sp = jnp.where(mask_p, _mm_nt(q, kp_ref[0, :, sl]), NEG_BIG)
        m = jnp.maximum(jnp.max(sc, axis=-1, keepdims=True), jnp.max(sp, axis=-1, keepdims=True))
        pc = jnp.exp(sc - m)
        pp = jnp.exp(sp - m)
        den = jnp.sum(pc, axis=-1, keepdims=True) + jnp.sum(pp, axis=-1, keepdims=True)
        o = (_mm(pc.astype(BF16), vc_ref[0, :, sl]) + _mm(pp.astype(BF16), vp_ref[0, :, sl])) / den
        o_ref[0, :, sl] = o.astype(BF16)
        lse_tile = jnp.where(lane == hg, m + jnp.log(den), lse_tile)
    lse_ref[0] = lse_tile


def _attn_core(qkv, group, dil, batch, seq):
    c = qkv.shape[1]
    gw = HEADS_PER_GROUP * LANES
    ncol = c // gw
    kofs = ncol // 3
    ls = seq // dil
    nb = ls // SPAN
    qkv3 = qkv.reshape(batch, ls, dil * c)

    def spec(ofs, prev):
        if prev:
            return pl.BlockSpec((1, SPAN, gw), lambda b, r, n: (b, jnp.maximum(n - 1, 0), r * ncol + ofs + group))
        return pl.BlockSpec((1, SPAN, gw), lambda b, r, n: (b, n, r * ncol + ofs + group))

    o, lse = pl.pallas_call(
        _attn_core_body,
        grid=(batch, dil, nb),
        in_specs=[spec(0, False), spec(kofs, True), spec(kofs, False), spec(2 * kofs, True), spec(2 * kofs, False)],
        out_specs=[pl.BlockSpec((1, SPAN, gw), lambda b, r, n: (b, n, r)),
                   pl.BlockSpec((1, SPAN, LANES), lambda b, r, n: (b, n, r))],
        out_shape=[jax.ShapeDtypeStruct((batch, ls, dil * gw), BF16),
                   jax.ShapeDtypeStruct((batch, ls, dil * LANES), F32)],
        compiler_params=_cparams(3),
        name=f"attn_core_d{dil}",
    )(qkv3, qkv3, qkv3, qkv3, qkv3)
    return o.reshape(batch * seq, gw), lse.reshape(batch * seq, LANES)


def _attn_out_body(x_ref, o0_ref, o1_ref, o2_ref, l0_ref, l1_ref, l2_ref, w_ref, out_ref):
    ls = [l0_ref[...], l1_ref[...], l2_ref[...]]
    m = jnp.maximum(jnp.maximum(ls[0], ls[1]), ls[2])
    es = [jnp.exp(l - m) for l in ls]
    inv = 1.0 / (es[0] + es[1] + es[2])
    acc = x_ref[...]
    for g, o_ref in enumerate((o0_ref, o1_ref, o2_ref)):
        alpha = es[g] * inv
        og = o_ref[...].astype(F32)
        scaled = jnp.concatenate(
            [og[:, hg * LANES:(hg + 1) * LANES] * alpha[:, hg:hg + 1] for hg in range(HEADS_PER_GROUP)], axis=1)
        acc = acc + _mm(scaled.astype(BF16), w_ref[g])
    out_ref[...] = acc


def _attn_out(x, os_, lses, w):
    n, d = x.shape
    gw = HEADS_PER_GROUP * LANES
    tm = TOKEN_TILE
    row = lambda width: pl.BlockSpec((tm, width), lambda i: (i, 0))
    return pl.pallas_call(
        _attn_out_body,
        grid=(n // tm,),
        in_specs=[row(d)] + [row(gw)] * 3 + [row(LANES)] * 3 + [_resident(w.shape)],
        out_specs=row(d),
        out_shape=jax.ShapeDtypeStruct((n, d), F32),
        compiler_params=_cparams(1),
        name="attn_out",
    )(x, *os_, *lses, w)


def _segmented_cumsum(x, axis):
    pos = lax.broadcasted_iota(jnp.int32, x.shape, axis) % GDN_CHUNK
    s = 1
    while s < GDN_CHUNK:
        x = x + jnp.where(pos >= s, pltpu.roll(x, s, axis), 0.0)
        s *= 2
    return x


def _gdn_in_body(x_ref, nw_ref, w_ref, wba_ref, wbat_ref, cw_ref, prm_ref, prmt_ref,
                 q_ref, k_ref, v_ref, z_ref, gb_ref, gbt_ref, pre_s, *, tiles_per_seq, kd, vd):
    i = pl.program_id(0)
    tm = x_ref.shape[0]
    halo = 8
    conv_dim = 2 * kd + vd
    h = _rms(x_ref[...], nw_ref[...]).astype(BF16)
    y = _mm(h, w_ref[...])
    z_ref[...] = y[:, conv_dim:].astype(BF16)

    @pl.when(i % tiles_per_seq == 0)
    def _():
        pre_s[0:halo, :] = jnp.zeros((halo, conv_dim), F32)

    pre_s[halo:halo + tm, :] = y[:, :conv_dim]
    for cb in range(conv_dim // LANES):
        sl = slice(cb * LANES, (cb + 1) * LANES)
        acc = pre_s[halo:halo + tm, sl] * cw_ref[GDN_CONV - 1:GDN_CONV, sl]
        for j in range(GDN_CONV - 1):
            ofs = halo - (GDN_CONV - 1) + j
            acc = acc + pre_s[ofs:ofs + tm, sl] * cw_ref[j:j + 1, sl]
        a = _silu(acc)
        if cb * LANES < 2 * kd:
            a = a * lax.rsqrt(jnp.sum(a * a, axis=-1, keepdims=True) + L2_EPS)
            if cb * LANES < kd:
                q_ref[:, sl] = (a * (LANES ** -0.5)).astype(BF16)
            else:
                k_ref[:, cb * LANES - kd:(cb + 1) * LANES - kd] = a.astype(BF16)
        else:
            v_ref[:, cb * LANES - 2 * kd:(cb + 1) * LANES - 2 * kd] = a.astype(BF16)
    pre_s[0:halo, :] = pre_s[tm:tm + halo, :]

    hbk = GDN_HEAD_BLOCK
    for hg in range(GDN_V_HEADS // hbk):
        ba = _mm(h, wba_ref[hg])
        lane = lax.broadcasted_iota(jnp.int32, ba.shape, 1)
        beta = 1.0 / (1.0 + jnp.exp(-ba))
        g = -jnp.exp(prm_ref[hg, 0:1, :]) * _softplus(ba + prm_ref[hg, 1:2, :])
        gb_ref[hg] = jnp.where(lane < hbk, beta, _segmented_cumsum(g, 0))
    bat = _mm_nt(wbat_ref[...], h)
    for p in range(tm // LANES):
        blk = bat[:, p * LANES:(p + 1) * LANES]
        rowi = lax.broadcasted_iota(jnp.int32, blk.shape, 0)
        beta = 1.0 / (1.0 + jnp.exp(-blk))
        g = -jnp.exp(prmt_ref[0]) * _softplus(blk + prmt_ref[1])
        res = jnp.where(rowi % 8 < hbk, beta, _segmented_cumsum(g, 1))
        for half in range(LANES // GDN_CHUNK):
            gbt_ref[p * (LANES // GDN_CHUNK) + half] = res[:, half * GDN_CHUNK:(half + 1) * GDN_CHUNK]


def _gdn_in(x, nw, w, wba, wbat, cw, prm, prmt, seq, kd, vd):
    n, d = x.shape
    tm = TOKEN_TILE
    conv_dim = 2 * kd + vd
    nhg = GDN_V_HEADS // GDN_HEAD_BLOCK
    row = lambda width: pl.BlockSpec((tm, width), lambda i: (i, 0))
    return pl.pallas_call(
        functools.partial(_gdn_in_body, tiles_per_seq=seq // tm, kd=kd, vd=vd),
        grid=(n // tm,),
        in_specs=[row(d), pl.BlockSpec((1, d), lambda i: (0, 0)), _resident(w.shape), _resident(wba.shape),
                  _resident(wbat.shape), _resident(cw.shape), _resident(prm.shape), _resident(prmt.shape)],
        out_specs=[row(kd), row(kd), row(vd), row(vd),
                   pl.BlockSpec((nhg, tm, LANES), lambda i: (0, i, 0)),
                   pl.BlockSpec((tm // GDN_CHUNK, 8 * nhg, GDN_CHUNK), lambda i: (i, 0, 0))],
        out_shape=[jax.ShapeDtypeStruct((n, kd), BF16), jax.ShapeDtypeStruct((n, kd), BF16),
                   jax.ShapeDtypeStruct((n, vd), BF16), jax.ShapeDtypeStruct((n, vd), BF16),
                   jax.ShapeDtypeStruct((nhg, n, LANES), F32),
                   jax.ShapeDtypeStruct((n // GDN_CHUNK, 8 * nhg, GDN_CHUNK), F32)],
        scratch_shapes=[pltpu.VMEM((tm + 8, conv_dim), F32)],
        compiler_params=_cparams(1),
        name="gdn_in",
    )(x, nw, w, wba, wbat, cw, prm, prmt)


def _gdn_core_body(q_ref, k_ref, v_ref, z_ref, gb_ref, gbt_ref, nw_ref, o_ref, st_s):
    cs = GDN_CHUNK
    hbk = GDN_HEAD_BLOCK
    nchunks = q_ref.shape[1] // cs

    @pl.when(pl.program_id(2) == 0)
    def _():
        st_s[...] = jnp.zeros(st_s.shape, F32)

    row = lax.broadcasted_iota(jnp.int32, (cs, cs), 0)
    col = lax.broadcasted_iota(jnp.int32, (cs, cs), 1)
    incl = row >= col
    strict = row > col
    eye = jnp.where(row == col, 1.0, 0.0).astype(F32)
    nw = nw_ref[...]

    for c in range(nchunks):
        rs = slice(c * cs, (c + 1) * cs)
        gbc = gb_ref[0, 0, rs, :]
        for hk in range(hbk // 2):
            ks = slice(hk * LANES, (hk + 1) * LANES)
            qb = q_ref[0, rs, ks]
            kb = k_ref[0, rs, ks]
            qf = qb.astype(F32)
            kf = kb.astype(F32)
            qkk = _mm_nt(jnp.concatenate([qb, kb], axis=0), kb)
            qk = qkk[:cs]
            kk = qkk[cs:]
            for jj in range(2):
                j = 2 * hk + jj
                vs = slice(j * LANES, (j + 1) * LANES)
                beta = gbc[:, j:j + 1]
                gc = gbc[:, hbk + j:hbk + j + 1]
                gcr = gbt_ref[c, hbk + j:hbk + j + 1, :]
                dec = jnp.where(incl, jnp.exp(jnp.where(incl, gc - gcr, 0.0)), 0.0)
                m = jnp.where(strict, -(beta * kk * dec), 0.0)
                tinv = eye + m
                for _ in range(5):
                    mb = m.astype(BF16)
                    m = _mm(mb, mb)
                    tinv = tinv + _mm(tinv.astype(BF16), m.astype(BF16))
                eg = jnp.exp(gc)
                vf = v_ref[0, rs, vs].astype(F32)
                rhs = jnp.concatenate([vf * beta, kf * (beta * eg)], axis=1).astype(BF16)
                sol = _mm(tinv.astype(BF16), rhs)
                u = sol[:, :LANES]
                w = sol[:, LANES:]
                gl = gc[cs - 1:cs, :]
                qd = qf * eg
                kdec = (kf * jnp.exp(gl - gc)).astype(BF16)
                state = st_s[j]
                sb = state.astype(BF16)
                both = _mm(jnp.concatenate([w, qd], axis=0).astype(BF16), sb)
                vb = (u - both[:cs]).astype(BF16)
                o = both[cs:] + _mm((qk * dec).astype(BF16), vb)
                st_s[j] = state * jnp.exp(gl) + _mm_tn(kdec, vb)
                on = o * lax.rsqrt(jnp.mean(o * o, axis=-1, keepdims=True) + RMS_EPS) * nw
                o_ref[0, rs, vs] = (on * _silu(z_ref[0, rs, vs].astype(F32))).astype(BF16)


def _gdn_core(q, k, v, z, gb, gbt, nw, batch, seq):
    n, kd = q.shape
    vd = v.shape[1]
    hbk = GDN_HEAD_BLOCK
    tb = GDN_TIME_BLOCK
    nt = seq // tb
    nhg = GDN_V_HEADS // hbk
    q3, k3 = q.reshape(batch, seq, kd), k.reshape(batch, seq, kd)
    v3, z3 = v.reshape(batch, seq, vd), z.reshape(batch, seq, vd)
    gb4 = gb.reshape(nhg, batch, seq, LANES)
    kw = hbk // 2 * LANES
    vw = hbk * LANES
    o = pl.pallas_call(
        _gdn_core_body,
        grid=(batch, nhg, nt),
        in_specs=[pl.BlockSpec((1, tb, kw), lambda b, g, t: (b, t, g)),
                  pl.BlockSpec((1, tb, kw), lambda b, g, t: (b, t, g)),
                  pl.BlockSpec((1, tb, vw), lambda b, g, t: (b, t, g)),
                  pl.BlockSpec((1, tb, vw), lambda b, g, t: (b, t, g)),
                  pl.BlockSpec((1, 1, tb, LANES), lambda b, g, t: (g, b, t, 0)),
                  pl.BlockSpec((tb // GDN_CHUNK, 8, GDN_CHUNK), lambda b, g, t: (b * nt + t, g, 0)),
                  pl.BlockSpec((1, LANES), lambda b, g, t: (0, 0))],
        out_specs=pl.BlockSpec((1, tb, vw), lambda b, g, t: (b, t, g)),
        out_shape=jax.ShapeDtypeStruct((batch, seq, vd), BF16),
        scratch_shapes=[pltpu.VMEM((hbk, LANES, LANES), F32)],
        compiler_params=_cparams(3),
        name="gdn_core",
    )(q3, k3, v3, z3, gb4, gbt, nw)
    return o.reshape(n, vd)


def _proj_out_body(x_ref, a_ref, w_ref, o_ref):
    o_ref[...] = x_ref[...] + _mm(a_ref[...], w_ref[...])


def _proj_out(x, a, w):
    n, d = x.shape
    tm = TOKEN_TILE
    return pl.pallas_call(
        _proj_out_body,
        grid=(n // tm,),
        in_specs=[pl.BlockSpec((tm, d), lambda i: (i, 0)), pl.BlockSpec((tm, a.shape[1]), lambda i: (i, 0)),
                  _resident(w.shape)],
        out_specs=pl.BlockSpec((tm, d), lambda i: (i, 0)),
        out_shape=jax.ShapeDtypeStruct((n, d), F32),
        compiler_params=_cparams(1),
        name="gdn_out",
    )(x, a, w)


def _rope_tables(seq):
    inv_freq = 1.0 / (ROPE_THETA ** (jnp.arange(0, LANES, 2, dtype=F32) / LANES))
    ang = jnp.arange(seq, dtype=F32)[:, None] * inv_freq[None, :]
    cos, sin = jnp.cos(ang), jnp.sin(ang)
    return jnp.concatenate([cos, cos], axis=-1), jnp.concatenate([-sin, sin], axis=-1)


def _gate_layouts(w_in, a_log, dt_bias, conv_dim, vd):
    hbk = GDN_HEAD_BLOCK
    nhg = GDN_V_HEADS // hbk
    d = w_in.shape[0]
    wb = w_in[:, conv_dim + vd:conv_dim + vd + GDN_V_HEADS].reshape(d, nhg, hbk)
    wa = w_in[:, conv_dim + vd + GDN_V_HEADS:].reshape(d, nhg, hbk)
    grp = jnp.concatenate([wb, wa], axis=-1)
    wba = jnp.pad(grp.transpose(1, 0, 2), ((0, 0), (0, 0), (0, LANES - 2 * hbk))).astype(BF16)
    wbat = grp.reshape(d, nhg * 2 * hbk).T.astype(BF16)
    zeros = jnp.zeros((nhg, hbk), F32)
    al = jnp.concatenate([zeros, a_log.reshape(nhg, hbk)], axis=-1)
    dt = jnp.concatenate([zeros, dt_bias.reshape(nhg, hbk)], axis=-1)
    prm = jnp.pad(jnp.stack([al, dt], axis=1), ((0, 0), (0, 6), (0, LANES - 2 * hbk)))
    prmt = jnp.stack([jnp.broadcast_to(al.reshape(-1, 1), (nhg * 2 * hbk, LANES)),
                      jnp.broadcast_to(dt.reshape(-1, 1), (nhg * 2 * hbk, LANES))])
    return wba, wbat, prm, prmt


def kernel(x, norm_w, ffn_w_in, ffn_w_out, attn_w_in, attn_w_out, gdn_w_in, gdn_conv_w, gdn_a_log,
           gdn_dt_bias, gdn_norm_w, gdn_w_out, final_norm_w):
    batch, seq, d = x.shape
    depth = norm_w.shape[0]
    dff = ffn_w_out.shape[2]
    kd = GDN_K_HEADS * LANES
    vd = GDN_V_HEADS * LANES
    conv_dim = 2 * kd + vd
    gw = HEADS_PER_GROUP * LANES
    assert seq % TOKEN_TILE == 0 and seq % GDN_TIME_BLOCK == 0
    assert all(w // dl == SPAN and seq % (dl * SPAN) == 0 for w, dl in DIL_PAIRS)

    cosf, sinf = _rope_tables(seq)
    xf = x.reshape(batch * seq, d)
    fw = final_norm_w.reshape(1, d)
    ia = ib = 0
    for i in range(depth):
        def ffn(xf, j, final):
            wi = ffn_w_in[i, j].astype(BF16)
            return _ffn(xf, norm_w[i, 2 * j].reshape(1, d), wi[:, :dff], wi[:, dff:],
                        ffn_w_out[i, j].astype(BF16), fw, final)

        xf = ffn(xf, 0, False)
        nw = norm_w[i, 1].reshape(1, d)
        if i % 2 == 0:
            qkv = _attn_in(xf, nw, attn_w_in[ia].astype(BF16), cosf, sinf, seq)
            outs = [_attn_core(qkv, g, dl, batch, seq) for g, (_, dl) in enumerate(DIL_PAIRS)]
            w_out = attn_w_out[ia].astype(BF16).reshape(len(DIL_PAIRS), gw, d)
            xf = _attn_out(xf, [o for o, _ in outs], [l for _, l in outs], w_out)
            ia += 1
        else:
            w_in = gdn_w_in[ib]
            wba, wbat, prm, prmt = _gate_layouts(w_in, gdn_a_log[ib], gdn_dt_bias[ib], conv_dim, vd)
            q, k, v, z, gb, gbt = _gdn_in(xf, nw, w_in[:, :conv_dim + vd].astype(BF16), wba, wbat,
                                          gdn_conv_w[ib], prm, prmt, seq, kd, vd)
            on = _gdn_core(q, k, v, z, gb, gbt, gdn_norm_w[ib].reshape(1, LANES), batch, seq)
            xf = _proj_out(xf, on, gdn_w_out[ib].astype(BF16))
            ib += 1
        xf = ffn(xf, 1, i == depth - 1)
    return xf.reshape(batch, seq, d)
```

```python
import functools

import jax
import jax.numpy as jnp
from jax import lax
from jax.experimental import pallas as pl
from jax.experimental.pallas import tpu as pltpu

F32, BF16 = jnp.float32, jnp.bfloat16

RMS_EPS = 1e-6
L2_EPS = 1e-6
LANES = 128
ROPE_THETA = 10000.0
DIL_PAIRS = ((128, 1), (512, 4), (2048, 16))
HEADS_PER_GROUP = 4
SPAN = 128
GDN_K_HEADS = 8
GDN_V_HEADS = 16
GDN_CHUNK = 64
GDN_CONV = 4
GDN_TIME_BLOCK = 128
TOKEN_TILE = 512
ATTN_ROWS = 512
VMEM_LIMIT_BYTES = 56 * 1024 * 1024
NEG_BIG = -1e30


def _cparams(n_axes):
    return pltpu.CompilerParams(dimension_semantics=("arbitrary",) * n_axes,
                                vmem_limit_bytes=VMEM_LIMIT_BYTES)


def _resident(shape):
    nd = len(shape)
    return pl.BlockSpec(shape, lambda *_: (0,) * nd, pipeline_mode=pl.Buffered(1))


def _mm(a, b):
    return jnp.dot(a, b, preferred_element_type=F32)


def _mm_nt(a, b):
    return lax.dot_general(a, b, (((1,), (1,)), ((), ())), preferred_element_type=F32)


def _mm_tn(a, b):
    return lax.dot_general(a, b, (((0,), (0,)), ((), ())), preferred_element_type=F32)


def _rms(x, w):
    ms = jnp.mean(x * x, axis=-1, keepdims=True)
    return x * lax.rsqrt(ms + RMS_EPS) * w


def _silu(x):
    return x * (1.0 / (1.0 + jnp.exp(-x)))


def _softplus(x):
    return jnp.maximum(x, 0.0) + jnp.log1p(jnp.exp(-jnp.abs(x)))


def _ffn_body(x_ref, nw_ref, wg_ref, wu_ref, wo_ref, fw_ref, o_ref, *, final):
    x = x_ref[...]
    h = _rms(x, nw_ref[...]).astype(BF16)
    g = _mm(h, wg_ref[...])
    u = _mm(h, wu_ref[...])
    a = (_silu(g) * u).astype(BF16)
    y = x + 0.5 * _mm(a, wo_ref[...])
    if final:
        y = _rms(y, fw_ref[...])
    o_ref[...] = y


def _ffn(x, nw, wg, wu, wo, fw, final):
    n, d = x.shape
    f = wg.shape[1]
    tm = TOKEN_TILE
    row = pl.BlockSpec((tm, d), lambda i: (i, 0))
    vec = pl.BlockSpec((1, d), lambda i: (0, 0))
    return pl.pallas_call(
        functools.partial(_ffn_body, final=final),
        grid=(n // tm,),
        in_specs=[row, vec, _resident((d, f)), _resident((d, f)), _resident((f, d)), vec],
        out_specs=row,
        out_shape=jax.ShapeDtypeStruct((n, d), F32),
        compiler_params=_cparams(1),
        name="ffn",
    )(x, nw, wg, wu, wo, fw)


def _attn_in_body(x_ref, nw_ref, w_ref, cos_ref, sin_ref, o_ref, *, n_heads, scale):
    h = _rms(x_ref[...], nw_ref[...]).astype(BF16)
    y = _mm(h, w_ref[...])
    cosf = cos_ref[...]
    sinf = sin_ref[...]
    for hd in range(3 * n_heads):
        blk = y[:, hd * LANES:(hd + 1) * LANES]
        if hd < 2 * n_heads:
            blk = blk * cosf + pltpu.roll(blk, LANES // 2, 1) * sinf
        if hd < n_heads:
            blk = blk * scale
        o_ref[hd] = blk


def _attn_in(x, nw, w, cosf, sinf, seq):
    n, d = x.shape
    c = w.shape[1]
    tm = TOKEN_TILE
    tiles_per_seq = seq // tm
    return pl.pallas_call(
        functools.partial(_attn_in_body, n_heads=c // (3 * LANES), scale=LANES ** -0.5),
        grid=(n // tm,),
        in_specs=[pl.BlockSpec((tm, d), lambda i: (i, 0)),
                  pl.BlockSpec((1, d), lambda i: (0, 0)),
                  _resident((d, c)),
                  pl.BlockSpec((tm, LANES), lambda i: (i % tiles_per_seq, 0)),
                  pl.BlockSpec((tm, LANES), lambda i: (i % tiles_per_seq, 0))],
        out_specs=pl.BlockSpec((c // LANES, tm, LANES), lambda i: (0, i, 0)),
        out_shape=jax.ShapeDtypeStruct((c // LANES, n, LANES), F32),
        compiler_params=_cparams(1),
        name="attn_in",
    )(x, nw, w, cosf, sinf)


def _attn_core_body(q_ref, k_ref, v_ref, o_ref, lse_ref, kp_s, vp_s, *, dil, nsub):
    n = pl.program_id(1)
    hd = pl.program_id(2)
    win = SPAN * dil

    @pl.when(hd == 0)
    def _():
        lse_ref[...] = jnp.zeros(lse_ref.shape, F32)

    @pl.when(n == 0)
    def _():
        kp_s[hd] = jnp.zeros((win, LANES), F32)
        vp_s[hd] = jnp.zeros((win, LANES), F32)

    def rows(r, i):
        start = i * win + r
        return slice(start, start + SPAN) if dil == 1 else pl.ds(start, SPAN, stride=dil)

    qi = lax.broadcasted_iota(jnp.int32, (SPAN, SPAN), 0)
    kj = lax.broadcasted_iota(jnp.int32, (SPAN, SPAN), 1)
    mask_c = kj <= qi
    mask_p = kj >= qi
    mask_p0 = jnp.logical_and(mask_p, n > 0)
    lane = lax.broadcasted_iota(jnp.int32, (SPAN, LANES), 1)

    blocks = [(r, i) for i in range(nsub) for r in range(dil)]
    scs, sps, vcs, vps = [], [], [], []
    for r, i in blocks:
        q = q_ref[0, 0, rows(r, i), :].astype(BF16)
        kc = k_ref[0, 0, rows(r, i), :].astype(BF16)
        vcs.append(v_ref[0, 0, rows(r, i), :].astype(BF16))
        if i == 0:
            kp = kp_s[hd, rows(r, 0), :].astype(BF16)
            vps.append(vp_s[hd, rows(r, 0), :].astype(BF16))
        else:
            kp = k_ref[0, 0, rows(r, i - 1), :].astype(BF16)
            vps.append(v_ref[0, 0, rows(r, i - 1), :].astype(BF16))
        scs.append(jnp.where(mask_c, _mm_nt(q, kc), NEG_BIG))
        sps.append(jnp.where(mask_p0 if i == 0 else mask_p, _mm_nt(q, kp), NEG_BIG))
    pcs, pps, dens, lses = [], [], [], []
    for sc, sp in zip(scs, sps):
        m = jnp.maximum(jnp.max(sc, axis=-1, keepdims=True), jnp.max(sp, axis=-1, keepdims=True))
        pc = jnp.exp(sc - m)
        pp = jnp.exp(sp - m)
        den = jnp.sum(pc, axis=-1, keepdims=True) + jnp.sum(pp, axis=-1, keepdims=True)
        pcs.append(pc.astype(BF16))
        pps.append(pp.astype(BF16))
        dens.append(den)
        lses.append(m + jnp.log(den))
    for b, (r, i) in enumerate(blocks):
        o_ref[0, 0, rows(r, i), :] = (_mm(pcs[b], vcs[b]) + _mm(pps[b], vps[b])) / dens[b]
        lse_ref[0, rows(r, i), :] = jnp.where(lane == hd, lses[b], lse_ref[0, rows(r, i), :])
    last = nsub * win - win
    kp_s[hd] = k_ref[0, 0, last:last + win, :]
    vp_s[hd] = v_ref[0, 0, last:last + win, :]


def _attn_core(qkv, group, dil, batch, seq):
    nh = qkv.shape[0] // 3
    hpg = HEADS_PER_GROUP
    nsub = max(1, ATTN_ROWS // (SPAN * dil))
    rows = nsub * SPAN * dil
    qkv4 = qkv.reshape(3 * nh, batch, seq, LANES)

    def spec(which):
        return pl.BlockSpec((1, 1, rows, LANES), lambda b, n, h: (which * nh + group * hpg + h, b, n, 0))

    o, lse = pl.pallas_call(
        functools.partial(_attn_core_body, dil=dil, nsub=nsub),
        grid=(batch, seq // rows, hpg),
        in_specs=[spec(0), spec(1), spec(2)],
        out_specs=[pl.BlockSpec((1, 1, rows, LANES), lambda b, n, h: (h, b, n, 0)),
                   pl.BlockSpec((1, rows, LANES), lambda b, n, h: (b, n, 0))],
        out_shape=[jax.ShapeDtypeStruct((hpg, batch, seq, LANES), F32),
                   jax.ShapeDtypeStruct((batch, seq, LANES), F32)],
        scratch_shapes=[pltpu.VMEM((hpg, SPAN * dil, LANES), F32), pltpu.VMEM((hpg, SPAN * dil, LANES), F32)],
        compiler_params=_cparams(3),
        name=f"attn_core_d{dil}",
    )(qkv4, qkv4, qkv4)
    return o.reshape(hpg, batch * seq, LANES), lse.reshape(batch * seq, LANES)


def _attn_out_body(x_ref, o0_ref, o1_ref, o2_ref, l0_ref, l1_ref, l2_ref, w_ref, out_ref):
    ls = [l0_ref[...], l1_ref[...], l2_ref[...]]
    m = jnp.maximum(jnp.maximum(ls[0], ls[1]), ls[2])
    es = [jnp.exp(l - m) for l in ls]
    inv = 1.0 / (es[0] + es[1] + es[2])
    acc = x_ref[...]
    for g, o_ref in enumerate((o0_ref, o1_ref, o2_ref)):
        alpha = es[g] * inv
        scaled = jnp.concatenate([o_ref[hg] * alpha[:, hg:hg + 1] for hg in range(HEADS_PER_GROUP)], axis=1)
        acc = acc + _mm(scaled.astype(BF16), w_ref[g])
    out_ref[...] = acc


def _attn_out(x, os_, lses, w):
    n, d = x.shape
    tm = TOKEN_TILE
    row = lambda width: pl.BlockSpec((tm, width), lambda i: (i, 0))
    heads = pl.BlockSpec((HEADS_PER_GROUP, tm, LANES), lambda i: (0, i, 0))
    return pl.pallas_call(
        _attn_out_body,
        grid=(n // tm,),
        in_specs=[row(d)] + [heads] * 3 + [row(LANES)] * 3 + [_resident(w.shape)],
        out_specs=row(d),
        out_shape=jax.ShapeDtypeStruct((n, d), F32),
        compiler_params=_cparams(1),
        name="attn_out",
    )(x, *os_, *lses, w)


def _segmented_cumsum(x, axis):
    pos = lax.broadcasted_iota(jnp.int32, x.shape, axis) % GDN_CHUNK
    s = 1
    while s < GDN_CHUNK:
        x = x + jnp.where(pos >= s, pltpu.roll(x, s, axis), 0.0)
        s *= 2
    return x


def _gdn_in_body(x_ref, nw_ref, w_ref, wba_ref, wat_ref, cw_ref, prm_ref, prmt_ref,
                 q_ref, k_ref, v_ref, z_ref, gb_ref, gbt_ref, pre_s, *, tiles_per_seq, kd, vd):
    i = pl.program_id(0)
    tm = x_ref.shape[0]
    halo = 8
    conv_dim = 2 * kd + vd
    h = _rms(x_ref[...], nw_ref[...]).astype(BF16)
    y = _mm(h, w_ref[...])
    z_ref[...] = y[:, conv_dim:].astype(BF16)

    @pl.when(i % tiles_per_seq == 0)
    def _():
        pre_s[0:halo, :] = jnp.zeros((halo, conv_dim), F32)

    pre_s[halo:halo + tm, :] = y[:, :conv_dim]
    for cb in range(conv_dim // LANES):
        sl = slice(cb * LANES, (cb + 1) * LANES)
        acc = pre_s[halo:halo + tm, sl] * cw_ref[GDN_CONV - 1:GDN_CONV, sl]
        for j in range(GDN_CONV - 1):
            ofs = halo - (GDN_CONV - 1) + j
            acc = acc + pre_s[ofs:ofs + tm, sl] * cw_ref[j:j + 1, sl]
        a = _silu(acc)
        if cb * LANES < 2 * kd:
            a = a * lax.rsqrt(jnp.sum(a * a, axis=-1, keepdims=True) + L2_EPS)
            if cb * LANES < kd:
                q_ref[:, sl] = (a * (LANES ** -0.5)).astype(BF16)
            else:
                k_ref[:, cb * LANES - kd:(cb + 1) * LANES - kd] = a.astype(BF16)
        else:
            v_ref[:, cb * LANES - 2 * kd:(cb + 1) * LANES - 2 * kd] = a.astype(BF16)
    pre_s[0:halo, :] = pre_s[tm:tm + halo, :]

    nvh = GDN_V_HEADS
    ba = _mm(h, wba_ref[...])
    lane = lax.broadcasted_iota(jnp.int32, ba.shape, 1)
    beta = 1.0 / (1.0 + jnp.exp(-ba))
    g = -jnp.exp(prm_ref[0:1, :]) * _softplus(ba + prm_ref[1:2, :])
    gb_ref[...] = jnp.where(lane < nvh, beta, _segmented_cumsum(g, 0))
    at = _mm_nt(wat_ref[...], h)
    cpl = LANES // GDN_CHUNK
    for p in range(tm // LANES):
        blk = at[:, p * LANES:(p + 1) * LANES]
        cum = _segmented_cumsum(-jnp.exp(prmt_ref[0]) * _softplus(blk + prmt_ref[1]), 1)
        ev, od = cum[:nvh // 2], cum[nvh // 2:]
        left = lax.broadcasted_iota(jnp.int32, ev.shape, 1) < GDN_CHUNK
        gbt_ref[cpl * p] = jnp.where(left, ev, pltpu.roll(od, GDN_CHUNK, 1))
        gbt_ref[cpl * p + 1] = jnp.where(left, pltpu.roll(ev, GDN_CHUNK, 1), od)


def _gdn_in(x, nw, w, wba, wat, cw, prm, prmt, seq, kd, vd):
    n, d = x.shape
    tm = TOKEN_TILE
    conv_dim = 2 * kd + vd
    row = lambda width: pl.BlockSpec((tm, width), lambda i: (i, 0))
    return pl.pallas_call(
        functools.partial(_gdn_in_body, tiles_per_seq=seq // tm, kd=kd, vd=vd),
        grid=(n // tm,),
        in_specs=[row(d), pl.BlockSpec((1, d), lambda i: (0, 0)), _resident(w.shape), _resident(wba.shape),
                  _resident(wat.shape), _resident(cw.shape), _resident(prm.shape), _resident(prmt.shape)],
        out_specs=[row(kd), row(kd), row(vd), row(vd), row(LANES),
                   pl.BlockSpec((tm // GDN_CHUNK, GDN_K_HEADS, LANES), lambda i: (i, 0, 0))],
        out_shape=[jax.ShapeDtypeStruct((n, kd), BF16), jax.ShapeDtypeStruct((n, kd), BF16),
                   jax.ShapeDtypeStruct((n, vd), BF16), jax.ShapeDtypeStruct((n, vd), BF16),
                   jax.ShapeDtypeStruct((n, LANES), F32),
                   jax.ShapeDtypeStruct((n // GDN_CHUNK, GDN_K_HEADS, LANES), F32)],
        scratch_shapes=[pltpu.VMEM((tm + 8, conv_dim), F32)],
        compiler_params=_cparams(1),
        name="gdn_in",
    )(x, nw, w, wba, wat, cw, prm, prmt)


def _gdn_core_body(q_ref, k_ref, v_ref, z_ref, gb_ref, gbt_ref, nw_ref, o_ref, st_s):
    cs = GDN_CHUNK
    nchunks = q_ref.shape[1] // cs
    npairs = GDN_K_HEADS
    gofs = GDN_V_HEADS

    @pl.when(pl.program_id(1) == 0)
    def _():
        st_s[...] = jnp.zeros(st_s.shape, F32)

    row = lax.broadcasted_iota(jnp.int32, (cs, LANES), 0)
    lane = lax.broadcasted_iota(jnp.int32, (cs, LANES), 1)
    colp = lane % cs
    left = lane < cs
    incl = row >= colp
    strict = row > colp
    eye2 = jnp.where(row == colp, 1.0, 0.0).astype(F32)
    nw = nw_ref[...]

    def blockdiag(x):
        return jnp.concatenate([jnp.where(left, x, 0.0), jnp.where(left, 0.0, x)], axis=0).astype(BF16)

    def side(x, s):
        return x[:, :cs] if s == 0 else pltpu.roll(x, cs, 1)[:, :cs]

    pairs = [(c, hk) for c in range(nchunks) for hk in range(npairs)]
    xs, ps, qkds = [], [], []
    for c, hk in pairs:
        rs = slice(c * cs, (c + 1) * cs)
        ks = slice(hk * LANES, (hk + 1) * LANES)
        gbc = gb_ref[0, rs, :]
        qb = q_ref[0, rs, ks]
        kb = k_ref[0, rs, ks]
        kq = _mm_nt(jnp.concatenate([qb, kb], axis=0), jnp.concatenate([kb, kb], axis=0))
        ja, jb = 2 * hk, 2 * hk + 1
        beta2 = jnp.where(left, gbc[:, ja:ja + 1], gbc[:, jb:jb + 1])
        gc2 = jnp.where(left, gbc[:, gofs + ja:gofs + ja + 1], gbc[:, gofs + jb:gofs + jb + 1])
        dec2 = jnp.where(incl, jnp.exp(jnp.where(incl, gc2 - gbt_ref[c, hk:hk + 1, :], 0.0)), 0.0)
        x = jnp.where(strict, -(beta2 * kq[cs:] * dec2), 0.0)
        xs.append(x)
        ps.append(eye2 + x)
        qkds.append(kq[:cs] * dec2)

    bds = [blockdiag(x) for x in xs]
    for _ in range(5):
        xs = [_mm(x.astype(BF16), bd) for x, bd in zip(xs, bds)]
        bds = [blockdiag(x) for x in xs]
        ps = [p + _mm(p.astype(BF16), bd) for p, bd in zip(ps, bds)]

    us, wqs, kdecs, cdecs, qkdh = {}, {}, {}, {}, {}
    for i, (c, hk) in enumerate(pairs):
        rs = slice(c * cs, (c + 1) * cs)
        ks = slice(hk * LANES, (hk + 1) * LANES)
        gbc = gb_ref[0, rs, :]
        qf = q_ref[0, rs, ks].astype(F32)
        kf = k_ref[0, rs, ks].astype(F32)
        for s in range(2):
            j = 2 * hk + s
            vs = slice(j * LANES, (j + 1) * LANES)
            beta = gbc[:, j:j + 1]
            gc = gbc[:, gofs + j:gofs + j + 1]
            eg = jnp.exp(gc)
            gl = gc[cs - 1:cs, :]
            rhs = jnp.concatenate([v_ref[0, rs, vs].astype(F32) * beta, kf * (beta * eg)], axis=1).astype(BF16)
            sol = _mm(side(ps[i], s).astype(BF16), rhs)
            us[c, j] = sol[:, :LANES]
            wqs[c, j] = jnp.concatenate([sol[:, LANES:], qf * eg], axis=0).astype(BF16)
            kdecs[c, j] = (kf * jnp.exp(gl - gc)).astype(BF16)
            cdecs[c, j] = jnp.exp(gl)
            qkdh[c, j] = side(qkds[i], s).astype(BF16)

    for c in range(nchunks):
        rs = slice(c * cs, (c + 1) * cs)
        states = [st_s[j] for j in range(GDN_V_HEADS)]
        boths = [_mm(wqs[c, j], states[j].astype(BF16)) for j in range(GDN_V_HEADS)]
        for j in range(GDN_V_HEADS):
            vs = slice(j * LANES, (j + 1) * LANES)
            vb = (us[c, j] - boths[j][:cs]).astype(BF16)
            o = boths[j][cs:] + _mm(qkdh[c, j], vb)
            st_s[j] = states[j] * cdecs[c, j] + _mm_tn(kdecs[c, j], vb)
            on = o * lax.rsqrt(jnp.mean(o * o, axis=-1, keepdims=True) + RMS_EPS) * nw
            o_ref[0, rs, vs] = (on * _silu(z_ref[0, rs, vs].astype(F32))).astype(BF16)


def _gdn_core(q, k, v, z, gb, gbt, nw, batch, seq):
    n, kd = q.shape
    vd = v.shape[1]
    tb = GDN_TIME_BLOCK
    nt = seq // tb
    tok = lambda width: pl.BlockSpec((1, tb, width), lambda b, t: (b, t, 0))
    o = pl.pallas_call(
        _gdn_core_body,
        grid=(batch, nt),
        in_specs=[tok(kd), tok(kd), tok(vd), tok(vd), tok(LANES),
                  pl.BlockSpec((tb // GDN_CHUNK, GDN_K_HEADS, LANES), lambda b, t: (b * nt + t, 0, 0)),
                  pl.BlockSpec((1, LANES), lambda b, t: (0, 0))],
        out_specs=tok(vd),
        out_shape=jax.ShapeDtypeStruct((batch, seq, vd), BF16),
        scratch_shapes=[pltpu.VMEM((GDN_V_HEADS, LANES, LANES), F32)],
        compiler_params=_cparams(2),
        name="gdn_core",
    )(q.reshape(batch, seq, kd), k.reshape(batch, seq, kd), v.reshape(batch, seq, vd), z.reshape(batch, seq, vd),
      gb.reshape(batch, seq, LANES), gbt, nw)
    return o.reshape(n, vd)


def _proj_out_body(x_ref, a_ref, w_ref, o_ref):
    o_ref[...] = x_ref[...] + _mm(a_ref[...], w_ref[...])


def _proj_out(x, a, w):
    n, d = x.shape
    tm = TOKEN_TILE
    return pl.pallas_call(
        _proj_out_body,
        grid=(n // tm,),
        in_specs=[pl.BlockSpec((tm, d), lambda i: (i, 0)), pl.BlockSpec((tm, a.shape[1]), lambda i: (i, 0)),
                  _resident(w.shape)],
        out_specs=pl.BlockSpec((tm, d), lambda i: (i, 0)),
        out_shape=jax.ShapeDtypeStruct((n, d), F32),
        compiler_params=_cparams(1),
        name="gdn_out",
    )(x, a, w)


def _rope_tables(seq):
    inv_freq = 1.0 / (ROPE_THETA ** (jnp.arange(0, LANES, 2, dtype=F32) / LANES))
    ang = jnp.arange(seq, dtype=F32)[:, None] * inv_freq[None, :]
    cos, sin = jnp.cos(ang), jnp.sin(ang)
    return jnp.concatenate([cos, cos], axis=-1), jnp.concatenate([-sin, sin], axis=-1)


def _gate_layouts(w_in, a_log, dt_bias, conv_dim, vd):
    nvh = GDN_V_HEADS
    wb = w_in[:, conv_dim + vd:conv_dim + vd + nvh]
    wa = w_in[:, conv_dim + vd + nvh:]
    wba = jnp.pad(jnp.concatenate([wb, wa], axis=1), ((0, 0), (0, LANES - 2 * nvh))).astype(BF16)
    wat = jnp.concatenate([wa[:, 0::2], wa[:, 1::2]], axis=1).T.astype(BF16)
    pad = lambda p: jnp.pad(p.reshape(1, nvh), ((0, 0), (nvh, LANES - 2 * nvh)))
    prm = jnp.pad(jnp.concatenate([pad(a_log), pad(dt_bias)], axis=0), ((0, 6), (0, 0)))
    eo = lambda p: jnp.broadcast_to(jnp.concatenate([p[0::2], p[1::2]]).reshape(nvh, 1), (nvh, LANES))
    prmt = jnp.stack([eo(a_log), eo(dt_bias)])
    return wba, wat, prm, prmt


def kernel(x, norm_w, ffn_w_in, ffn_w_out, attn_w_in, attn_w_out, gdn_w_in, gdn_conv_w, gdn_a_log,
           gdn_dt_bias, gdn_norm_w, gdn_w_out, final_norm_w):
    batch, seq, d = x.shape
    depth = norm_w.shape[0]
    dff = ffn_w_out.shape[2]
    kd = GDN_K_HEADS * LANES
    vd = GDN_V_HEADS * LANES
    conv_dim = 2 * kd + vd
    gw = HEADS_PER_GROUP * LANES
    assert seq % TOKEN_TILE == 0 and seq % GDN_TIME_BLOCK == 0
    assert all(w // dl == SPAN and seq % (dl * SPAN) == 0 for w, dl in DIL_PAIRS)

    cosf, sinf = _rope_tables(seq)
    xf = x.reshape(batch * seq, d)
    fw = final_norm_w.reshape(1, d)
    ia = ib = 0
    for i in range(depth):
        def ffn(xf, j, final):
            wi = ffn_w_in[i, j].astype(BF16)
            return _ffn(xf, norm_w[i, 2 * j].reshape(1, d), wi[:, :dff], wi[:, dff:],
                        ffn_w_out[i, j].astype(BF16), fw, final)

        xf = ffn(xf, 0, False)
        nw = norm_w[i, 1].reshape(1, d)
        if i % 2 == 0:
            qkv = _attn_in(xf, nw, attn_w_in[ia].astype(BF16), cosf, sinf, seq)
            outs = [_attn_core(qkv, g, dl, batch, seq) for g, (_, dl) in enumerate(DIL_PAIRS)]
            w_out = attn_w_out[ia].astype(BF16).reshape(len(DIL_PAIRS), gw, d)
            xf = _attn_out(xf, [o for o, _ in outs], [l for _, l in outs], w_out)
            ia += 1
        else:
            w_in = gdn_w_in[ib]
            wba, wat, prm, prmt = _gate_layouts(w_in, gdn_a_log[ib], gdn_dt_bias[ib], conv_dim, vd)
            q, k, v, z, gb, gbt = _gdn_in(xf, nw, w_in[:, :conv_dim + vd].astype(BF16), wba, wat,
                                          gdn_conv_w[ib], prm, prmt, seq, kd, vd)
            on = _gdn_core(q, k, v, z, gb, gbt, gdn_norm_w[ib].reshape(1, LANES), batch, seq)
            xf = _proj_out(xf, on, gdn_w_out[ib].astype(BF16))
            ib += 1
        xf = ffn(xf, 1, i == depth - 1)
    return xf.reshape(batch, seq, d)
```

```python
import functools

import jax
import jax.numpy as jnp
from jax import lax
from jax.experimental import pallas as pl
from jax.experimental.pallas import tpu as pltpu

F32, BF16 = jnp.float32, jnp.bfloat16

RMS_EPS = 1e-6
L2_EPS = 1e-6
LANES = 128
ROPE_THETA = 10000.0
DIL_PAIRS = ((128, 1), (512, 4), (2048, 16))
HEADS_PER_GROUP = 4
SPAN = 128
GDN_K_HEADS = 8
GDN_V_HEADS = 16
GDN_CHUNK = 64
GDN_CONV = 4
GDN_COL_BLOCK = 256
GDN_TIME_BLOCK = 128
TOKEN_TILE = 512
ATTN_ROWS = 2048
VMEM_LIMIT_BYTES = 56 * 1024 * 1024
NEG_BIG = -1e30


def _cparams(n_axes):
    return pltpu.CompilerParams(dimension_semantics=("arbitrary",) * n_axes,
                                vmem_limit_bytes=VMEM_LIMIT_BYTES)


def _resident(shape):
    nd = len(shape)
    return pl.BlockSpec(shape, lambda *_: (0,) * nd, pipeline_mode=pl.Buffered(1))


def _mm(a, b):
    return jnp.dot(a, b, preferred_element_type=F32)


def _mm_nt(a, b):
    return lax.dot_general(a, b, (((1,), (1,)), ((), ())), preferred_element_type=F32)


def _mm_tn(a, b):
    return lax.dot_general(a, b, (((0,), (0,)), ((), ())), preferred_element_type=F32)


def _rms(x, w):
    ms = jnp.mean(x * x, axis=-1, keepdims=True)
    return x * lax.rsqrt(ms + RMS_EPS) * w


def _silu(x):
    return x * (1.0 / (1.0 + jnp.exp(-x)))


def _softplus(x):
    return jnp.maximum(x, 0.0) + jnp.log1p(jnp.exp(-jnp.abs(x)))


def _ffn_body(x_ref, nw_ref, wg_ref, wu_ref, wo_ref, fw_ref, o_ref, *, final):
    x = x_ref[...]
    h = _rms(x, nw_ref[...]).astype(BF16)
    g = _mm(h, wg_ref[...])
    u = _mm(h, wu_ref[...])
    a = (_silu(g) * u).astype(BF16)
    y = x + 0.5 * _mm(a, wo_ref[...])
    if final:
        y = _rms(y, fw_ref[...])
    o_ref[...] = y


def _ffn(x, nw, wg, wu, wo, fw, final):
    n, d = x.shape
    f = wg.shape[1]
    tm = TOKEN_TILE
    row = pl.BlockSpec((tm, d), lambda i: (i, 0))
    vec = pl.BlockSpec((1, d), lambda i: (0, 0))
    return pl.pallas_call(
        functools.partial(_ffn_body, final=final),
        grid=(n // tm,),
        in_specs=[row, vec, _resident((d, f)), _resident((d, f)), _resident((f, d)), vec],
        out_specs=row,
        out_shape=jax.ShapeDtypeStruct((n, d), F32),
        compiler_params=_cparams(1),
        name="ffn",
    )(x, nw, wg, wu, wo, fw)


def _attn_in_body(x_ref, nw_ref, w_ref, cos_ref, sin_ref, o_ref, *, n_heads, scale):
    h = _rms(x_ref[...], nw_ref[...]).astype(BF16)
    y = _mm(h, w_ref[...])
    cosf = cos_ref[...]
    sinf = sin_ref[...]
    for hd in range(3 * n_heads):
        blk = y[:, hd * LANES:(hd + 1) * LANES]
        if hd < 2 * n_heads:
            blk = blk * cosf + pltpu.roll(blk, LANES // 2, 1) * sinf
        if hd < n_heads:
            blk = blk * scale
        o_ref[hd] = blk


def _attn_in(x, nw, w, cosf, sinf, seq):
    n, d = x.shape
    c = w.shape[1]
    tm = TOKEN_TILE
    tiles_per_seq = seq // tm
    return pl.pallas_call(
        functools.partial(_attn_in_body, n_heads=c // (3 * LANES), scale=LANES ** -0.5),
        grid=(n // tm,),
        in_specs=[pl.BlockSpec((tm, d), lambda i: (i, 0)),
                  pl.BlockSpec((1, d), lambda i: (0, 0)),
                  _resident((d, c)),
                  pl.BlockSpec((tm, LANES), lambda i: (i % tiles_per_seq, 0)),
                  pl.BlockSpec((tm, LANES), lambda i: (i % tiles_per_seq, 0))],
        out_specs=pl.BlockSpec((c // LANES, tm, LANES), lambda i: (0, i, 0)),
        out_shape=jax.ShapeDtypeStruct((c // LANES, n, LANES), F32),
        compiler_params=_cparams(1),
        name="attn_in",
    )(x, nw, w, cosf, sinf)


def _attn_core_body(q_ref, k_ref, v_ref, o_ref, lse_ref, kp_s, vp_s, *, dil, nsub):
    n = pl.program_id(1)
    hd = pl.program_id(2)
    win = SPAN * dil

    @pl.when(hd == 0)
    def _():
        lse_ref[...] = jnp.zeros(lse_ref.shape, F32)

    @pl.when(n == 0)
    def _():
        kp_s[hd] = jnp.zeros((win, LANES), F32)
        vp_s[hd] = jnp.zeros((win, LANES), F32)

    def rows(r, i):
        start = i * win + r
        return slice(start, start + SPAN) if dil == 1 else pl.ds(start, SPAN, stride=dil)

    qi = lax.broadcasted_iota(jnp.int32, (SPAN, SPAN), 0)
    kj = lax.broadcasted_iota(jnp.int32, (SPAN, SPAN), 1)
    mask_c = kj <= qi
    mask_p = kj >= qi
    mask_p0 = jnp.logical_and(mask_p, n > 0)
    lane = lax.broadcasted_iota(jnp.int32, (SPAN, LANES), 1)

    blocks = [(r, i) for i in range(nsub) for r in range(dil)]
    scs, sps, vcs, vps = [], [], [], []
    for r, i in blocks:
        q = q_ref[0, 0, rows(r, i), :].astype(BF16)
        kc = k_ref[0, 0, rows(r, i), :].astype(BF16)
        vcs.append(v_ref[0, 0, rows(r, i), :].astype(BF16))
        if i == 0:
            kp = kp_s[hd, rows(r, 0), :].astype(BF16)
            vps.append(vp_s[hd, rows(r, 0), :].astype(BF16))
        else:
            kp = k_ref[0, 0, rows(r, i - 1), :].astype(BF16)
            vps.append(v_ref[0, 0, rows(r, i - 1), :].astype(BF16))
        scs.append(jnp.where(mask_c, _mm_nt(q, kc), NEG_BIG))
        sps.append(jnp.where(mask_p0 if i == 0 else mask_p, _mm_nt(q, kp), NEG_BIG))
    pcs, pps, dens, lses = [], [], [], []
    for sc, sp in zip(scs, sps):
        m = jnp.maximum(jnp.max(sc, axis=-1, keepdims=True), jnp.max(sp, axis=-1, keepdims=True))
        pc = jnp.exp(sc - m)
        pp = jnp.exp(sp - m)
        den = jnp.sum(pc, axis=-1, keepdims=True) + jnp.sum(pp, axis=-1, keepdims=True)
        pcs.append(pc.astype(BF16))
        pps.append(pp.astype(BF16))
        dens.append(den)
        lses.append(m + jnp.log(den))
    for b, (r, i) in enumerate(blocks):
        o_ref[0, 0, rows(r, i), :] = (_mm(pcs[b], vcs[b]) + _mm(pps[b], vps[b])) / dens[b]
        lse_ref[0, rows(r, i), :] = jnp.where(lane == hd, lses[b], lse_ref[0, rows(r, i), :])
    last = nsub * win - win
    kp_s[hd] = k_ref[0, 0, last:last + win, :]
    vp_s[hd] = v_ref[0, 0, last:last + win, :]


def _attn_core(qkv, group, dil, batch, seq):
    nh = qkv.shape[0] // 3
    hpg = HEADS_PER_GROUP
    nsub = max(1, ATTN_ROWS // (SPAN * dil))
    rows = nsub * SPAN * dil
    qkv4 = qkv.reshape(3 * nh, batch, seq, LANES)

    def spec(which):
        return pl.BlockSpec((1, 1, rows, LANES), lambda b, n, h: (which * nh + group * hpg + h, b, n, 0))

    o, lse = pl.pallas_call(
        functools.partial(_attn_core_body, dil=dil, nsub=nsub),
        grid=(batch, seq // rows, hpg),
        in_specs=[spec(0), spec(1), spec(2)],
        out_specs=[pl.BlockSpec((1, 1, rows, LANES), lambda b, n, h: (h, b, n, 0)),
                   pl.BlockSpec((1, rows, LANES), lambda b, n, h: (b, n, 0))],
        out_shape=[jax.ShapeDtypeStruct((hpg, batch, seq, LANES), F32),
                   jax.ShapeDtypeStruct((batch, seq, LANES), F32)],
        scratch_shapes=[pltpu.VMEM((hpg, SPAN * dil, LANES), F32), pltpu.VMEM((hpg, SPAN * dil, LANES), F32)],
        compiler_params=_cparams(3),
        name=f"attn_core_d{dil}",
    )(qkv4, qkv4, qkv4)
    return o.reshape(hpg, batch * seq, LANES), lse.reshape(batch * seq, LANES)


def _attn_out_body(x_ref, o0_ref, o1_ref, o2_ref, l0_ref, l1_ref, l2_ref, w_ref, out_ref):
    ls = [l0_ref[...], l1_ref[...], l2_ref[...]]
    m = jnp.maximum(jnp.maximum(ls[0], ls[1]), ls[2])
    es = [jnp.exp(l - m) for l in ls]
    inv = 1.0 / (es[0] + es[1] + es[2])
    acc = x_ref[...]
    for g, o_ref in enumerate((o0_ref, o1_ref, o2_ref)):
        alpha = es[g] * inv
        scaled = jnp.concatenate([o_ref[hg] * alpha[:, hg:hg + 1] for hg in range(HEADS_PER_GROUP)], axis=1)
        acc = acc + _mm(scaled.astype(BF16), w_ref[g])
    out_ref[...] = acc


def _attn_out(x, os_, lses, w):
    n, d = x.shape
    tm = TOKEN_TILE
    row = lambda width: pl.BlockSpec((tm, width), lambda i: (i, 0))
    heads = pl.BlockSpec((HEADS_PER_GROUP, tm, LANES), lambda i: (0, i, 0))
    return pl.pallas_call(
        _attn_out_body,
        grid=(n // tm,),
        in_specs=[row(d)] + [heads] * 3 + [row(LANES)] * 3 + [_resident(w.shape)],
        out_specs=row(d),
        out_shape=jax.ShapeDtypeStruct((n, d), F32),
        compiler_params=_cparams(1),
        name="attn_out",
    )(x, *os_, *lses, w)


def _segmented_cumsum(x, axis):
    pos = lax.broadcasted_iota(jnp.int32, x.shape, axis) % GDN_CHUNK
    s = 1
    while s < GDN_CHUNK:
        x = x + jnp.where(pos >= s, pltpu.roll(x, s, axis), 0.0)
        s *= 2
    return x


def _gdn_in_body(x_ref, nw_ref, w_ref, wba_ref, wat_ref, cw_ref, prm_ref, prmt_ref,
                 q_ref, k_ref, v_ref, z_ref, gb_ref, gbt_ref, pre_s, halo_s, stage_s, *, tiles_per_seq, kd, vd):
    i = pl.program_id(0)
    tm = x_ref.shape[0]
    halo = 8
    conv_dim = 2 * kd + vd
    cbw = GDN_COL_BLOCK
    h = _rms(x_ref[...], nw_ref[...]).astype(BF16)

    @pl.when(i % tiles_per_seq == 0)
    def _():
        halo_s[...] = jnp.zeros(halo_s.shape, F32)

    def conv_block(cb):
        buf = cb % 2
        hrows = tm // 2
        for hh in range(cbw // LANES):
            c0 = cb * cbw + hh * LANES
            sl = slice(c0, c0 + LANES)
            taps = [cw_ref[j:j + 1, sl] for j in range(GDN_CONV)]
            first = halo - (GDN_CONV - 1)
            wins = [pre_s[buf, hh, pl.ds(first + k, hrows, stride=2), :] for k in range(GDN_CONV + 1)]
            for par in range(2):
                acc = wins[par] * taps[0]
                for j in range(1, GDN_CONV):
                    acc = acc + wins[par + j] * taps[j]
                a = _silu(acc)
                if c0 < 2 * kd:
                    r = lax.rsqrt(jnp.sum(a * a, axis=-1, keepdims=True) + L2_EPS)
                    a = a * (r * (LANES ** -0.5) if c0 < kd else r)
                stage_s[hh, pl.ds(par, hrows, stride=2), :] = a
            a = stage_s[hh].astype(BF16)
            if c0 < kd:
                q_ref[:, sl] = a
            elif c0 < 2 * kd:
                k_ref[:, c0 - kd:c0 - kd + LANES] = a
            else:
                v_ref[:, c0 - 2 * kd:c0 - 2 * kd + LANES] = a

    n_conv = conv_dim // cbw
    n_z = vd // cbw
    for cb in range(n_conv + 1):
        if cb < n_conv:
            cs_ = slice(cb * cbw, (cb + 1) * cbw)
            yb = _mm(h, w_ref[:, cs_])
            buf = cb % 2
            for hh in range(cbw // LANES):
                hs = slice(cb * cbw + hh * LANES, cb * cbw + (hh + 1) * LANES)
                pre_s[buf, hh, 0:halo, :] = halo_s[:, hs]
                pre_s[buf, hh, halo:halo + tm, :] = yb[:, hh * LANES:(hh + 1) * LANES]
            halo_s[:, cs_] = yb[tm - halo:, :]
        if cb * n_z // n_conv != (cb + 1) * n_z // n_conv and cb < n_conv:
            zb = cb * n_z // n_conv
            zs = slice(zb * cbw, (zb + 1) * cbw)
            z_ref[:, zs] = _mm(h, w_ref[:, conv_dim + zb * cbw:conv_dim + (zb + 1) * cbw]).astype(BF16)
        if cb > 0:
            conv_block(cb - 1)

    nvh = GDN_V_HEADS
    ba = _mm(h, wba_ref[...])
    lane = lax.broadcasted_iota(jnp.int32, ba.shape, 1)
    beta = 1.0 / (1.0 + jnp.exp(-ba))
    g = -jnp.exp(prm_ref[0:1, :]) * _softplus(ba + prm_ref[1:2, :])
    gb_ref[...] = jnp.where(lane < nvh, beta, _segmented_cumsum(g, 0))
    at = _mm_nt(wat_ref[...], h)
    cpl = LANES // GDN_CHUNK
    for p in range(tm // LANES):
        blk = at[:, p * LANES:(p + 1) * LANES]
        cum = _segmented_cumsum(-jnp.exp(prmt_ref[0]) * _softplus(blk + prmt_ref[1]), 1)
        ev, od = cum[:nvh // 2], cum[nvh // 2:]
        left = lax.broadcasted_iota(jnp.int32, ev.shape, 1) < GDN_CHUNK
        gbt_ref[cpl * p] = jnp.where(left, ev, pltpu.roll(od, GDN_CHUNK, 1))
        gbt_ref[cpl * p + 1] = jnp.where(left, pltpu.roll(ev, GDN_CHUNK, 1), od)


def _gdn_in(x, nw, w, wba, wat, cw, prm, prmt, seq, kd, vd):
    n, d = x.shape
    tm = TOKEN_TILE
    conv_dim = 2 * kd + vd
    row = lambda width: pl.BlockSpec((tm, width), lambda i: (i, 0))
    return pl.pallas_call(
        functools.partial(_gdn_in_body, tiles_per_seq=seq // tm, kd=kd, vd=vd),
        grid=(n // tm,),
        in_specs=[row(d), pl.BlockSpec((1, d), lambda i: (0, 0)), _resident(w.shape), _resident(wba.shape),
                  _resident(wat.shape), _resident(cw.shape), _resident(prm.shape), _resident(prmt.shape)],
        out_specs=[row(kd), row(kd), row(vd), row(vd), row(LANES),
                   pl.BlockSpec((tm // GDN_CHUNK, GDN_K_HEADS, LANES), lambda i: (i, 0, 0))],
        out_shape=[jax.ShapeDtypeStruct((n, kd), BF16), jax.ShapeDtypeStruct((n, kd), BF16),
                   jax.ShapeDtypeStruct((n, vd), BF16), jax.ShapeDtypeStruct((n, vd), BF16),
                   jax.ShapeDtypeStruct((n, LANES), F32),
                   jax.ShapeDtypeStruct((n // GDN_CHUNK, GDN_K_HEADS, LANES), F32)],
        scratch_shapes=[pltpu.VMEM((2, GDN_COL_BLOCK // LANES, tm + 8, LANES), F32),
                        pltpu.VMEM((8, conv_dim), F32),
                        pltpu.VMEM((GDN_COL_BLOCK // LANES, tm, LANES), F32)],
        compiler_params=_cparams(1),
        name="gdn_in",
    )(x, nw, w, wba, wat, cw, prm, prmt)


def _gdn_core_body(q_ref, k_ref, v_ref, z_ref, gb_ref, gbt_ref, nw_ref, o_ref, st_s):
    cs = GDN_CHUNK
    nchunks = q_ref.shape[1] // cs
    npairs = GDN_K_HEADS
    gofs = GDN_V_HEADS

    @pl.when(pl.program_id(1) == 0)
    def _():
        st_s[...] = jnp.zeros(st_s.shape, F32)

    row = lax.broadcasted_iota(jnp.int32, (cs, LANES), 0)
    lane = lax.broadcasted_iota(jnp.int32, (cs, LANES), 1)
    colp = lane % cs
    left = lane < cs
    incl = row >= colp
    strict = row > colp
    eye2 = jnp.where(row == colp, 1.0, 0.0).astype(F32)
    nw = nw_ref[...]

    def blockdiag(x):
        return jnp.concatenate([jnp.where(left, x, 0.0), jnp.where(left, 0.0, x)], axis=0).astype(BF16)

    def side(x, s):
        return x[:, :cs] if s == 0 else pltpu.roll(x, cs, 1)[:, :cs]

    pairs = [(c, hk) for c in range(nchunks) for hk in range(npairs)]
    xs, ps, qkds = [], [], []
    for c, hk in pairs:
        rs = slice(c * cs, (c + 1) * cs)
        ks = slice(hk * LANES, (hk + 1) * LANES)
        gbc = gb_ref[0, rs, :]
        qb = q_ref[0, rs, ks]
        kb = k_ref[0, rs, ks]
        kq = _mm_nt(jnp.concatenate([qb, kb], axis=0), jnp.concatenate([kb, kb], axis=0))
        ja, jb = 2 * hk, 2 * hk + 1
        beta2 = jnp.where(left, gbc[:, ja:ja + 1], gbc[:, jb:jb + 1])
        gc2 = jnp.where(left, gbc[:, gofs + ja:gofs + ja + 1], gbc[:, gofs + jb:gofs + jb + 1])
        dec2 = jnp.where(incl, jnp.exp(jnp.where(incl, gc2 - gbt_ref[c, hk:hk + 1, :], 0.0)), 0.0)
        x = jnp.where(strict, -(beta2 * kq[cs:] * dec2), 0.0)
        xs.append(x)
        ps.append(eye2 + x)
        qkds.append(kq[:cs] * dec2)

    bds = [blockdiag(x) for x in xs]
    for _ in range(5):
        xs = [_mm(x.astype(BF16), bd) for x, bd in zip(xs, bds)]
        bds = [blockdiag(x) for x in xs]
        ps = [p + _mm(p.astype(BF16), bd) for p, bd in zip(ps, bds)]

    us, wqs, kdecs, cdecs, qkdh = {}, {}, {}, {}, {}
    for i, (c, hk) in enumerate(pairs):
        rs = slice(c * cs, (c + 1) * cs)
        ks = slice(hk * LANES, (hk + 1) * LANES)
        gbc = gb_ref[0, rs, :]
        qf = q_ref[0, rs, ks].astype(F32)
        kf = k_ref[0, rs, ks].astype(F32)
        for s in range(2):
            j = 2 * hk + s
            vs = slice(j * LANES, (j + 1) * LANES)
            beta = gbc[:, j:j + 1]
            gc = gbc[:, gofs + j:gofs + j + 1]
            eg = jnp.exp(gc)
            gl = gc[cs - 1:cs, :]
            rhs = jnp.concatenate([v_ref[0, rs, vs].astype(F32) * beta, kf * (beta * eg)], axis=1).astype(BF16)
            sol = _mm(side(ps[i], s).astype(BF16), rhs)
            us[c, j] = sol[:, :LANES]
            wqs[c, j] = jnp.concatenate([sol[:, LANES:], qf * eg], axis=0).astype(BF16)
            kdecs[c, j] = (kf * jnp.exp(gl - gc)).astype(BF16)
            cdecs[c, j] = jnp.exp(gl)
            qkdh[c, j] = side(qkds[i], s).astype(BF16)

    for c in range(nchunks):
        rs = slice(c * cs, (c + 1) * cs)
        states = [st_s[j] for j in range(GDN_V_HEADS)]
        boths = [_mm(wqs[c, j], states[j].astype(BF16)) for j in range(GDN_V_HEADS)]
        for j in range(GDN_V_HEADS):
            vs = slice(j * LANES, (j + 1) * LANES)
            vb = (us[c, j] - boths[j][:cs]).astype(BF16)
            o = boths[j][cs:] + _mm(qkdh[c, j], vb)
            st_s[j] = states[j] * cdecs[c, j] + _mm_tn(kdecs[c, j], vb)
            on = o * lax.rsqrt(jnp.mean(o * o, axis=-1, keepdims=True) + RMS_EPS) * nw
            o_ref[0, rs, vs] = (on * _silu(z_ref[0, rs, vs].astype(F32))).astype(BF16)


def _gdn_core(q, k, v, z, gb, gbt, nw, batch, seq):
    n, kd = q.shape
    vd = v.shape[1]
    tb = GDN_TIME_BLOCK
    nt = seq // tb
    tok = lambda width: pl.BlockSpec((1, tb, width), lambda b, t: (b, t, 0))
    o = pl.pallas_call(
        _gdn_core_body,
        grid=(batch, nt),
        in_specs=[tok(kd), tok(kd), tok(vd), tok(vd), tok(LANES),
                  pl.BlockSpec((tb // GDN_CHUNK, GDN_K_HEADS, LANES), lambda b, t: (b * nt + t, 0, 0)),
                  pl.BlockSpec((1, LANES), lambda b, t: (0, 0))],
        out_specs=tok(vd),
        out_shape=jax.ShapeDtypeStruct((batch, seq, vd), BF16),
        scratch_shapes=[pltpu.VMEM((GDN_V_HEADS, LANES, LANES), F32)],
        compiler_params=_cparams(2),
        name="gdn_core",
    )(q.reshape(batch, seq, kd), k.reshape(batch, seq, kd), v.reshape(batch, seq, vd), z.reshape(batch, seq, vd),
      gb.reshape(batch, seq, LANES), gbt, nw)
    return o.reshape(n, vd)


def _proj_out_body(x_ref, a_ref, w_ref, o_ref):
    o_ref[...] = x_ref[...] + _mm(a_ref[...], w_ref[...])


def _proj_out(x, a, w):
    n, d = x.shape
    tm = TOKEN_TILE
    return pl.pallas_call(
        _proj_out_body,
        grid=(n // tm,),
        in_specs=[pl.BlockSpec((tm, d), lambda i: (i, 0)), pl.BlockSpec((tm, a.shape[1]), lambda i: (i, 0)),
                  _resident(w.shape)],
        out_specs=pl.BlockSpec((tm, d), lambda i: (i, 0)),
        out_shape=jax.ShapeDtypeStruct((n, d), F32),
        compiler_params=_cparams(1),
        name="gdn_out",
    )(x, a, w)


def _rope_tables(seq):
    inv_freq = 1.0 / (ROPE_THETA ** (jnp.arange(0, LANES, 2, dtype=F32) / LANES))
    ang = jnp.arange(seq, dtype=F32)[:, None] * inv_freq[None, :]
    cos, sin = jnp.cos(ang), jnp.sin(ang)
    return jnp.concatenate([cos, cos], axis=-1), jnp.concatenate([-sin, sin], axis=-1)


def _gate_layouts(w_in, a_log, dt_bias, conv_dim, vd):
    nvh = GDN_V_HEADS
    wb = w_in[:, conv_dim + vd:conv_dim + vd + nvh]
    wa = w_in[:, conv_dim + vd + nvh:]
    wba = jnp.pad(jnp.concatenate([wb, wa], axis=1), ((0, 0), (0, LANES - 2 * nvh))).astype(BF16)
    wat = jnp.concatenate([wa[:, 0::2], wa[:, 1::2]], axis=1).T.astype(BF16)
    pad = lambda p: jnp.pad(p.reshape(1, nvh), ((0, 0), (nvh, LANES - 2 * nvh)))
    prm = jnp.pad(jnp.concatenate([pad(a_log), pad(dt_bias)], axis=0), ((0, 6), (0, 0)))
    eo = lambda p: jnp.broadcast_to(jnp.concatenate([p[0::2], p[1::2]]).reshape(nvh, 1), (nvh, LANES))
    prmt = jnp.stack([eo(a_log), eo(dt_bias)])
    return wba, wat, prm, prmt


def kernel(x, norm_w, ffn_w_in, ffn_w_out, attn_w_in, attn_w_out, gdn_w_in, gdn_conv_w, gdn_a_log,
           gdn_dt_bias, gdn_norm_w, gdn_w_out, final_norm_w):
    batch, seq, d = x.shape
    depth = norm_w.shape[0]
    dff = ffn_w_out.shape[2]
    kd = GDN_K_HEADS * LANES
    vd = GDN_V_HEADS * LANES
    conv_dim = 2 * kd + vd
    gw = HEADS_PER_GROUP * LANES
    assert seq % TOKEN_TILE == 0 and seq % GDN_TIME_BLOCK == 0
    assert all(w // dl == SPAN and seq % (dl * SPAN) == 0 for w, dl in DIL_PAIRS)

    cosf, sinf = _rope_tables(seq)
    xf = x.reshape(batch * seq, d)
    fw = final_norm_w.reshape(1, d)
    ia = ib = 0
    for i in range(depth):
        def ffn(xf, j, final):
            wi = ffn_w_in[i, j].astype(BF16)
            return _ffn(xf, norm_w[i, 2 * j].reshape(1, d), wi[:, :dff], wi[:, dff:],
                        ffn_w_out[i, j].astype(BF16), fw, final)

        xf = ffn(xf, 0, False)
        nw = norm_w[i, 1].reshape(1, d)
        if i % 2 == 0:
            qkv = _attn_in(xf, nw, attn_w_in[ia].astype(BF16), cosf, sinf, seq)
            outs = [_attn_core(qkv, g, dl, batch, seq) for g, (_, dl) in enumerate(DIL_PAIRS)]
            w_out = attn_w_out[ia].astype(BF16).reshape(len(DIL_PAIRS), gw, d)
            xf = _attn_out(xf, [o for o, _ in outs], [l for _, l in outs], w_out)
            ia += 1
        else:
            w_in = gdn_w_in[ib]
            wba, wat, prm, prmt = _gate_layouts(w_in, gdn_a_log[ib], gdn_dt_bias[ib], conv_dim, vd)
            q, k, v, z, gb, gbt = _gdn_in(xf, nw, w_in[:, :conv_dim + vd].astype(BF16), wba, wat,
                                          gdn_conv_w[ib], prm, prmt, seq, kd, vd)
            on = _gdn_core(q, k, v, z, gb, gbt, gdn_norm_w[ib].reshape(1, LANES), batch, seq)
            xf = _proj_out(xf, on, gdn_w_out[ib].astype(BF16))
            ib += 1
        xf = ffn(xf, 1, i == depth - 1)
    return xf.reshape(batch, seq, d)
```

```python
import functools

import jax
import jax.numpy as jnp
from jax import lax
from jax.experimental import pallas as pl
from jax.experimental.pallas import tpu as pltpu

F32, BF16 = jnp.float32, jnp.bfloat16

RMS_EPS = 1e-6
L2_EPS = 1e-6
LANES = 128
ROPE_THETA = 10000.0
DIL_PAIRS = ((128, 1), (512, 4), (2048, 16))
HEADS_PER_GROUP = 4
SPAN = 128
GDN_K_HEADS = 8
GDN_V_HEADS = 16
GDN_CHUNK = 64
GDN_CONV = 4
GDN_COL_BLOCK = 256
GDN_TIME_BLOCK = 128
TOKEN_TILE = 512
FFN_TILE = 1024
FFN_COL_BLOCK = 256
ATTN_ROWS = 2048
VMEM_LIMIT_BYTES = 56 * 1024 * 1024
NEG_BIG = -1e30


def _cparams(n_axes, flags=None):
    return pltpu.CompilerParams(dimension_semantics=("arbitrary",) * n_axes,
                                vmem_limit_bytes=VMEM_LIMIT_BYTES, flags=flags)


def _resident(shape):
    nd = len(shape)
    return pl.BlockSpec(shape, lambda *_: (0,) * nd, pipeline_mode=pl.Buffered(1))


def _mm(a, b):
    return jnp.dot(a, b, preferred_element_type=F32)


def _mm_nt(a, b):
    return lax.dot_general(a, b, (((1,), (1,)), ((), ())), preferred_element_type=F32)


def _mm_tn(a, b):
    return lax.dot_general(a, b, (((0,), (0,)), ((), ())), preferred_element_type=F32)


def _rms(x, w):
    ms = jnp.mean(x * x, axis=-1, keepdims=True)
    return x * lax.rsqrt(ms + RMS_EPS) * w


def _silu(x):
    return x * (1.0 / (1.0 + jnp.exp(-x)))


def _softplus(x):
    return jnp.maximum(x, 0.0) + jnp.log1p(jnp.exp(-jnp.abs(x)))


def _ffn_body(x_ref, nw_ref, wi_ref, wo_ref, fw_ref, o_ref, *, final):
    x = x_ref[...]
    h = _rms(x, nw_ref[...]).astype(BF16)
    f = wo_ref.shape[0]
    ff = None
    for c in range(f // FFN_COL_BLOCK):
        fs = slice(c * FFN_COL_BLOCK, (c + 1) * FFN_COL_BLOCK)
        us = slice(f + c * FFN_COL_BLOCK, f + (c + 1) * FFN_COL_BLOCK)
        a = (_silu(_mm(h, wi_ref[:, fs])) * _mm(h, wi_ref[:, us])).astype(BF16)
        part = _mm(a, wo_ref[fs, :])
        ff = part if ff is None else ff + part
    y = x + 0.5 * ff
    if final:
        y = _rms(y, fw_ref[...])
    o_ref[...] = y


def _ffn(x, nw, wi, wo, fw, final):
    n, d = x.shape
    tm = FFN_TILE
    row = pl.BlockSpec((tm, d), lambda i: (i, 0))
    vec = pl.BlockSpec((1, d), lambda i: (0, 0))
    return pl.pallas_call(
        functools.partial(_ffn_body, final=final),
        grid=(n // tm,),
        in_specs=[row, vec, _resident(wi.shape), _resident(wo.shape), vec],
        out_specs=row,
        out_shape=jax.ShapeDtypeStruct((n, d), F32),
        compiler_params=_cparams(1),
        name="ffn",
    )(x, nw, wi, wo, fw)


def _attn_in_body(x_ref, nw_ref, w_ref, cos_ref, sin_ref, o_ref, *, n_heads, scale):
    h = _rms(x_ref[...], nw_ref[...]).astype(BF16)
    y = _mm(h, w_ref[...])
    cosf = cos_ref[...]
    sinf = sin_ref[...]
    for hd in range(3 * n_heads):
        blk = y[:, hd * LANES:(hd + 1) * LANES]
        if hd < 2 * n_heads:
            blk = blk * cosf + pltpu.roll(blk, LANES // 2, 1) * sinf
        if hd < n_heads:
            blk = blk * scale
        o_ref[hd] = blk


def _attn_in(x, nw, w, cosf, sinf, seq):
    n, d = x.shape
    c = w.shape[1]
    tm = TOKEN_TILE
    tiles_per_seq = seq // tm
    return pl.pallas_call(
        functools.partial(_attn_in_body, n_heads=c // (3 * LANES), scale=LANES ** -0.5),
        grid=(n // tm,),
        in_specs=[pl.BlockSpec((tm, d), lambda i: (i, 0)),
                  pl.BlockSpec((1, d), lambda i: (0, 0)),
                  _resident((d, c)),
                  pl.BlockSpec((tm, LANES), lambda i: (i % tiles_per_seq, 0)),
                  pl.BlockSpec((tm, LANES), lambda i: (i % tiles_per_seq, 0))],
        out_specs=pl.BlockSpec((c // LANES, tm, LANES), lambda i: (0, i, 0)),
        out_shape=jax.ShapeDtypeStruct((c // LANES, n, LANES), F32),
        compiler_params=_cparams(1),
        name="attn_in",
    )(x, nw, w, cosf, sinf)


def _attn_core_body(q_ref, k_ref, v_ref, o_ref, lse_ref, kp_s, vp_s, lse_s, tmp_s, *, dil, nsub):
    n = pl.program_id(1)
    hd = pl.program_id(2)
    win = SPAN * dil
    inner = 4 if dil > 4 else dil

    @pl.when(hd == 0)
    def _():
        lse_s[...] = jnp.zeros(lse_s.shape, F32)

    @pl.when(n == 0)
    def _():
        kp_s[hd] = jnp.zeros(kp_s.shape[1:], BF16)
        vp_s[hd] = jnp.zeros(vp_s.shape[1:], BF16)

    def rows(r, i):
        start = i * win + r
        return slice(start, start + SPAN) if dil == 1 else pl.ds(start, SPAN, stride=dil)

    def gather(t, ref, i):
        if dil <= inner:
            return [ref[0, 0, rows(r, i), :].astype(BF16) for r in range(dil)]
        outer = dil // inner
        for ri in range(inner):
            tmp_s[t, ri] = ref[0, 0, pl.ds(i * win + ri, win // inner, stride=inner), :]
        return [tmp_s[t, r % inner, pl.ds(r // inner, SPAN, stride=outer), :].astype(BF16) for r in range(dil)]

    qi = lax.broadcasted_iota(jnp.int32, (SPAN, SPAN), 0)
    kj = lax.broadcasted_iota(jnp.int32, (SPAN, SPAN), 1)
    mask_c = kj <= qi
    mask_p = kj >= qi
    mask_p0 = jnp.logical_and(mask_p, n > 0)
    lane = lax.broadcasted_iota(jnp.int32, (SPAN, LANES), 1)

    blocks, scs, sps, vcs, vps = [], [], [], [], []
    kprev = [kp_s[hd, r] for r in range(dil)]
    vprev = [vp_s[hd, r] for r in range(dil)]
    for i in range(nsub):
        qs, kcs, vws = gather(0, q_ref, i), gather(1, k_ref, i), gather(2, v_ref, i)
        for r in range(dil):
            blocks.append((r, i))
            scs.append(jnp.where(mask_c, _mm_nt(qs[r], kcs[r]), NEG_BIG))
            sps.append(jnp.where(mask_p0 if i == 0 else mask_p, _mm_nt(qs[r], kprev[r]), NEG_BIG))
            vcs.append(vws[r])
            vps.append(vprev[r])
        kprev, vprev = kcs, vws
    for r in range(dil):
        kp_s[hd, r] = kprev[r]
        vp_s[hd, r] = vprev[r]
    pcs, pps, dens, lses = [], [], [], []
    for sc, sp in zip(scs, sps):
        m = jnp.maximum(jnp.max(sc, axis=-1, keepdims=True), jnp.max(sp, axis=-1, keepdims=True))
        pc = jnp.exp(sc - m)
        pp = jnp.exp(sp - m)
        den = jnp.sum(pc, axis=-1, keepdims=True) + jnp.sum(pp, axis=-1, keepdims=True)
        pcs.append(pc.astype(BF16))
        pps.append(pp.astype(BF16))
        dens.append(den)
        lses.append(m + jnp.log(den))
    for b, (r, i) in enumerate(blocks):
        o_ref[0, 0, rows(r, i), :] = (_mm(pcs[b], vcs[b]) + _mm(pps[b], vps[b])) / dens[b]
        lse_s[b] = jnp.where(lane == hd, lses[b], lse_s[b])

    @pl.when(hd == pl.num_programs(2) - 1)
    def _():
        for b, (r, i) in enumerate(blocks):
            lse_ref[0, rows(r, i), :] = lse_s[b]


def _attn_core(qkv, group, dil, batch, seq):
    nh = qkv.shape[0] // 3
    hpg = HEADS_PER_GROUP
    nsub = max(1, ATTN_ROWS // (SPAN * dil))
    rows = nsub * SPAN * dil
    qkv4 = qkv.reshape(3 * nh, batch, seq, LANES)

    def spec(which):
        return pl.BlockSpec((1, 1, rows, LANES), lambda b, n, h: (which * nh + group * hpg + h, b, n, 0))

    o, lse = pl.pallas_call(
        functools.partial(_attn_core_body, dil=dil, nsub=nsub),
        grid=(batch, seq // rows, hpg),
        in_specs=[spec(0), spec(1), spec(2)],
        out_specs=[pl.BlockSpec((1, 1, rows, LANES), lambda b, n, h: (h, b, n, 0)),
                   pl.BlockSpec((1, rows, LANES), lambda b, n, h: (b, n, 0))],
        out_shape=[jax.ShapeDtypeStruct((hpg, batch, seq, LANES), F32),
                   jax.ShapeDtypeStruct((batch, seq, LANES), F32)],
        scratch_shapes=[pltpu.VMEM((hpg, dil, SPAN, LANES), BF16), pltpu.VMEM((hpg, dil, SPAN, LANES), BF16),
                        pltpu.VMEM((nsub * dil, SPAN, LANES), F32),
                        pltpu.VMEM((3, 4, SPAN * dil // 4, LANES), F32)],
        compiler_params=_cparams(3),
        name=f"attn_core_d{dil}",
    )(qkv4, qkv4, qkv4)
    return o.reshape(hpg, batch * seq, LANES), lse.reshape(batch * seq, LANES)


def _attn_out_body(x_ref, o0_ref, o1_ref, o2_ref, l0_ref, l1_ref, l2_ref, w_ref, out_ref):
    ls = [l0_ref[...], l1_ref[...], l2_ref[...]]
    m = jnp.maximum(jnp.maximum(ls[0], ls[1]), ls[2])
    es = [jnp.exp(l - m) for l in ls]
    inv = 1.0 / (es[0] + es[1] + es[2])
    acc = x_ref[...]
    for g, o_ref in enumerate((o0_ref, o1_ref, o2_ref)):
        alpha = es[g] * inv
        scaled = jnp.concatenate([o_ref[hg] * alpha[:, hg:hg + 1] for hg in range(HEADS_PER_GROUP)], axis=1)
        acc = acc + _mm(scaled.astype(BF16), w_ref[g])
    out_ref[...] = acc


def _attn_out(x, os_, lses, w):
    n, d = x.shape
    tm = TOKEN_TILE
    row = lambda width: pl.BlockSpec((tm, width), lambda i: (i, 0))
    heads = pl.BlockSpec((HEADS_PER_GROUP, tm, LANES), lambda i: (0, i, 0))
    return pl.pallas_call(
        _attn_out_body,
        grid=(n // tm,),
        in_specs=[row(d)] + [heads] * 3 + [row(LANES)] * 3 + [_resident(w.shape)],
        out_specs=row(d),
        out_shape=jax.ShapeDtypeStruct((n, d), F32),
        compiler_params=_cparams(1),
        name="attn_out",
    )(x, *os_, *lses, w)


def _segmented_cumsum(x, axis):
    pos = lax.broadcasted_iota(jnp.int32, x.shape, axis) % GDN_CHUNK
    s = 1
    while s < GDN_CHUNK:
        x = x + jnp.where(pos >= s, pltpu.roll(x, s, axis), 0.0)
        s *= 2
    return x


def _gdn_in_body(x_ref, nw_ref, w_ref, wba_ref, wat_ref, cw_ref, prm_ref, prmt_ref,
                 q_ref, k_ref, v_ref, z_ref, gb_ref, gbt_ref, pre_s, halo_s, stage_s, *, tiles_per_seq, kd, vd):
    i = pl.program_id(0)
    tm = x_ref.shape[0]
    halo = 8
    conv_dim = 2 * kd + vd
    cbw = GDN_COL_BLOCK
    h = _rms(x_ref[...], nw_ref[...]).astype(BF16)

    @pl.when(i % tiles_per_seq == 0)
    def _():
        halo_s[...] = jnp.zeros(halo_s.shape, F32)

    def conv_block(cb):
        buf = cb % 2
        hrows = tm // 2
        for hh in range(cbw // LANES):
            c0 = cb * cbw + hh * LANES
            sl = slice(c0, c0 + LANES)
            taps = [cw_ref[j:j + 1, sl] for j in range(GDN_CONV)]
            first = halo - (GDN_CONV - 1)
            wins = [pre_s[buf, hh, pl.ds(first + k, hrows, stride=2), :] for k in range(GDN_CONV + 1)]
            for par in range(2):
                acc = wins[par] * taps[0]
                for j in range(1, GDN_CONV):
                    acc = acc + wins[par + j] * taps[j]
                a = _silu(acc)
                if c0 < 2 * kd:
                    r = lax.rsqrt(jnp.sum(a * a, axis=-1, keepdims=True) + L2_EPS)
                    a = a * (r * (LANES ** -0.5) if c0 < kd else r)
                stage_s[hh, pl.ds(par, hrows, stride=2), :] = a
            a = stage_s[hh].astype(BF16)
            if c0 < kd:
                q_ref[:, sl] = a
            elif c0 < 2 * kd:
                k_ref[:, c0 - kd:c0 - kd + LANES] = a
            else:
                v_ref[:, c0 - 2 * kd:c0 - 2 * kd + LANES] = a

    nvh = GDN_V_HEADS
    ba = _mm(h, wba_ref[...])
    lane = lax.broadcasted_iota(jnp.int32, ba.shape, 1)
    beta = 1.0 / (1.0 + jnp.exp(-ba))
    g = -jnp.exp(prm_ref[0:1, :]) * _softplus(ba + prm_ref[1:2, :])
    gb_ref[...] = jnp.where(lane < nvh, beta, _segmented_cumsum(g, 0))
    at = _mm_nt(wat_ref[...], h)
    cpl = LANES // GDN_CHUNK
    for p in range(tm // LANES):
        blk = at[:, p * LANES:(p + 1) * LANES]
        cum = _segmented_cumsum(-jnp.exp(prmt_ref[0]) * _softplus(blk + prmt_ref[1]), 1)
        ev, od = cum[:nvh // 2], cum[nvh // 2:]
        left = lax.broadcasted_iota(jnp.int32, ev.shape, 1) < GDN_CHUNK
        gbt_ref[cpl * p] = jnp.where(left, ev, pltpu.roll(od, GDN_CHUNK, 1))
        gbt_ref[cpl * p + 1] = jnp.where(left, pltpu.roll(ev, GDN_CHUNK, 1), od)

    n_conv = conv_dim // cbw
    n_z = vd // cbw
    for cb in range(n_conv + 1):
        if cb < n_conv:
            cs_ = slice(cb * cbw, (cb + 1) * cbw)
            yb = _mm(h, w_ref[:, cs_])
            buf = cb % 2
            for hh in range(cbw // LANES):
                hs = slice(cb * cbw + hh * LANES, cb * cbw + (hh + 1) * LANES)
                pre_s[buf, hh, 0:halo, :] = halo_s[:, hs]
                pre_s[buf, hh, halo:halo + tm, :] = yb[:, hh * LANES:(hh + 1) * LANES]
            halo_s[:, cs_] = yb[tm - halo:, :]
        if cb > 0 and cb % (n_conv // n_z) == 0:
            zb = cb // (n_conv // n_z) - 1
            zs = slice(zb * cbw, (zb + 1) * cbw)
            z_ref[:, zs] = _mm(h, w_ref[:, conv_dim + zb * cbw:conv_dim + (zb + 1) * cbw]).astype(BF16)
        if cb > 0:
            conv_block(cb - 1)


def _gdn_in(x, nw, w, wba, wat, cw, prm, prmt, seq, kd, vd):
    n, d = x.shape
    tm = TOKEN_TILE
    conv_dim = 2 * kd + vd
    row = lambda width: pl.BlockSpec((tm, width), lambda i: (i, 0))
    return pl.pallas_call(
        functools.partial(_gdn_in_body, tiles_per_seq=seq // tm, kd=kd, vd=vd),
        grid=(n // tm,),
        in_specs=[row(d), pl.BlockSpec((1, d), lambda i: (0, 0)), _resident(w.shape), _resident(wba.shape),
                  _resident(wat.shape), _resident(cw.shape), _resident(prm.shape), _resident(prmt.shape)],
        out_specs=[row(kd), row(kd), row(vd), row(vd), row(LANES),
                   pl.BlockSpec((tm // GDN_CHUNK, GDN_K_HEADS, LANES), lambda i: (i, 0, 0))],
        out_shape=[jax.ShapeDtypeStruct((n, kd), BF16), jax.ShapeDtypeStruct((n, kd), BF16),
                   jax.ShapeDtypeStruct((n, vd), BF16), jax.ShapeDtypeStruct((n, vd), BF16),
                   jax.ShapeDtypeStruct((n, LANES), F32),
                   jax.ShapeDtypeStruct((n // GDN_CHUNK, GDN_K_HEADS, LANES), F32)],
        scratch_shapes=[pltpu.VMEM((2, GDN_COL_BLOCK // LANES, tm + 8, LANES), F32),
                        pltpu.VMEM((8, conv_dim), F32),
                        pltpu.VMEM((GDN_COL_BLOCK // LANES, tm, LANES), F32)],
        compiler_params=_cparams(1),
        name="gdn_in",
    )(x, nw, w, wba, wat, cw, prm, prmt)


def _gdn_core_body(q_ref, k_ref, v_ref, z_ref, gb_ref, gbt_ref, nw_ref, o_ref, st_s):
    cs = GDN_CHUNK
    nchunks = q_ref.shape[1] // cs
    npairs = GDN_K_HEADS
    gofs = GDN_V_HEADS

    @pl.when(pl.program_id(1) == 0)
    def _():
        st_s[...] = jnp.zeros(st_s.shape, F32)

    row = lax.broadcasted_iota(jnp.int32, (cs, LANES), 0)
    lane = lax.broadcasted_iota(jnp.int32, (cs, LANES), 1)
    colp = lane % cs
    left = lane < cs
    incl = row >= colp
    strict = row > colp
    eye2 = jnp.where(row == colp, 1.0, 0.0).astype(F32)
    nw = nw_ref[...]

    def blockdiag(x):
        return jnp.concatenate([jnp.where(left, x, 0.0), jnp.where(left, 0.0, x)], axis=0).astype(BF16)

    def side(x, s):
        return x[:, :cs] if s == 0 else pltpu.roll(x, cs, 1)[:, :cs]

    pairs = [(c, hk) for c in range(nchunks) for hk in range(npairs)]
    xs, ps, qkds = [], [], []
    for c, hk in pairs:
        rs = slice(c * cs, (c + 1) * cs)
        ks = slice(hk * LANES, (hk + 1) * LANES)
        gbc = gb_ref[0, rs, :]
        qb = q_ref[0, rs, ks]
        kb = k_ref[0, rs, ks]
        kq = _mm_nt(jnp.concatenate([qb, kb], axis=0), jnp.concatenate([kb, kb], axis=0))
        ja, jb = 2 * hk, 2 * hk + 1
        beta2 = jnp.where(left, gbc[:, ja:ja + 1], gbc[:, jb:jb + 1])
        gc2 = jnp.where(left, gbc[:, gofs + ja:gofs + ja + 1], gbc[:, gofs + jb:gofs + jb + 1])
        dec2 = jnp.where(incl, jnp.exp(jnp.where(incl, gc2 - gbt_ref[c, hk:hk + 1, :], 0.0)), 0.0)
        x = jnp.where(strict, -(beta2 * kq[cs:] * dec2), 0.0)
        xs.append(x)
        ps.append(eye2 + x)
        qkds.append(kq[:cs] * dec2)

    bds = [blockdiag(x) for x in xs]
    for _ in range(5):
        xs = [_mm(x.astype(BF16), bd) for x, bd in zip(xs, bds)]
        bds = [blockdiag(x) for x in xs]
        ps = [p + _mm(p.astype(BF16), bd) for p, bd in zip(ps, bds)]

    us, wqs, kdecs, cdecs, qkdh = {}, {}, {}, {}, {}
    for i, (c, hk) in enumerate(pairs):
        rs = slice(c * cs, (c + 1) * cs)
        ks = slice(hk * LANES, (hk + 1) * LANES)
        gbc = gb_ref[0, rs, :]
        qf = q_ref[0, rs, ks].astype(F32)
        kf = k_ref[0, rs, ks].astype(F32)
        for s in range(2):
            j = 2 * hk + s
            vs = slice(j * LANES, (j + 1) * LANES)
            beta = gbc[:, j:j + 1]
            gc = gbc[:, gofs + j:gofs + j + 1]
            eg = jnp.exp(gc)
            gl = gc[cs - 1:cs, :]
            rhs = jnp.concatenate([v_ref[0, rs, vs].astype(F32) * beta, kf * (beta * eg)], axis=1).astype(BF16)
            sol = _mm(side(ps[i], s).astype(BF16), rhs)
            us[c, j] = sol[:, :LANES]
            wqs[c, j] = jnp.concatenate([sol[:, LANES:], qf * eg], axis=0).astype(BF16)
            kdecs[c, j] = (kf * jnp.exp(gl - gc)).astype(BF16)
            cdecs[c, j] = jnp.exp(gl)
            qkdh[c, j] = side(qkds[i], s).astype(BF16)

    for c in range(nchunks):
        rs = slice(c * cs, (c + 1) * cs)
        states = [st_s[j] for j in range(GDN_V_HEADS)]
        boths = [_mm(wqs[c, j], states[j].astype(BF16)) for j in range(GDN_V_HEADS)]
        for j in range(GDN_V_HEADS):
            vs = slice(j * LANES, (j + 1) * LANES)
            vb = (us[c, j] - boths[j][:cs]).astype(BF16)
            o = boths[j][cs:] + _mm(qkdh[c, j], vb)
            st_s[j] = states[j] * cdecs[c, j] + _mm_tn(kdecs[c, j], vb)
            on = o * lax.rsqrt(jnp.mean(o * o, axis=-1, keepdims=True) + RMS_EPS) * nw
            o_ref[0, rs, vs] = (on * _silu(z_ref[0, rs, vs].astype(F32))).astype(BF16)


def _gdn_core(q, k, v, z, gb, gbt, nw, batch, seq):
    n, kd = q.shape
    vd = v.shape[1]
    tb = GDN_TIME_BLOCK
    nt = seq // tb
    tok = lambda width: pl.BlockSpec((1, tb, width), lambda b, t: (b, t, 0))
    o = pl.pallas_call(
        _gdn_core_body,
        grid=(batch, nt),
        in_specs=[tok(kd), tok(kd), tok(vd), tok(vd), tok(LANES),
                  pl.BlockSpec((tb // GDN_CHUNK, GDN_K_HEADS, LANES), lambda b, t: (b * nt + t, 0, 0)),
                  pl.BlockSpec((1, LANES), lambda b, t: (0, 0))],
        out_specs=tok(vd),
        out_shape=jax.ShapeDtypeStruct((batch, seq, vd), BF16),
        scratch_shapes=[pltpu.VMEM((GDN_V_HEADS, LANES, LANES), F32)],
        compiler_params=_cparams(2),
        name="gdn_core",
    )(q.reshape(batch, seq, kd), k.reshape(batch, seq, kd), v.reshape(batch, seq, vd), z.reshape(batch, seq, vd),
      gb.reshape(batch, seq, LANES), gbt, nw)
    return o.reshape(n, vd)


def _proj_out_body(x_ref, a_ref, w_ref, o_ref):
    o_ref[...] = x_ref[...] + _mm(a_ref[...], w_ref[...])


def _proj_out(x, a, w):
    n, d = x.shape
    tm = TOKEN_TILE
    return pl.pallas_call(
        _proj_out_body,
        grid=(n // tm,),
        in_specs=[pl.BlockSpec((tm, d), lambda i: (i, 0)), pl.BlockSpec((tm, a.shape[1]), lambda i: (i, 0)),
                  _resident(w.shape)],
        out_specs=pl.BlockSpec((tm, d), lambda i: (i, 0)),
        out_shape=jax.ShapeDtypeStruct((n, d), F32),
        compiler_params=_cparams(1),
        name="gdn_out",
    )(x, a, w)


def _rope_tables(seq):
    inv_freq = 1.0 / (ROPE_THETA ** (jnp.arange(0, LANES, 2, dtype=F32) / LANES))
    ang = jnp.arange(seq, dtype=F32)[:, None] * inv_freq[None, :]
    cos, sin = jnp.cos(ang), jnp.sin(ang)
    return jnp.concatenate([cos, cos], axis=-1), jnp.concatenate([-sin, sin], axis=-1)


def _gate_layouts(w_in, a_log, dt_bias, conv_dim, vd):
    nvh = GDN_V_HEADS
    wb = w_in[:, conv_dim + vd:conv_dim + vd + nvh]
    wa = w_in[:, conv_dim + vd + nvh:]
    wba = jnp.pad(jnp.concatenate([wb, wa], axis=1), ((0, 0), (0, LANES - 2 * nvh))).astype(BF16)
    wat = jnp.concatenate([wa[:, 0::2], wa[:, 1::2]], axis=1).T.astype(BF16)
    pad = lambda p: jnp.pad(p.reshape(1, nvh), ((0, 0), (nvh, LANES - 2 * nvh)))
    prm = jnp.pad(jnp.concatenate([pad(a_log), pad(dt_bias)], axis=0), ((0, 6), (0, 0)))
    eo = lambda p: jnp.broadcast_to(jnp.concatenate([p[0::2], p[1::2]]).reshape(nvh, 1), (nvh, LANES))
    prmt = jnp.stack([eo(a_log), eo(dt_bias)])
    return wba, wat, prm, prmt


def kernel(x, norm_w, ffn_w_in, ffn_w_out, attn_w_in, attn_w_out, gdn_w_in, gdn_conv_w, gdn_a_log,
           gdn_dt_bias, gdn_norm_w, gdn_w_out, final_norm_w):
    batch, seq, d = x.shape
    depth = norm_w.shape[0]
    kd = GDN_K_HEADS * LANES
    vd = GDN_V_HEADS * LANES
    conv_dim = 2 * kd + vd
    gw = HEADS_PER_GROUP * LANES
    assert seq % TOKEN_TILE == 0 and seq % GDN_TIME_BLOCK == 0
    assert all(w // dl == SPAN and seq % (dl * SPAN) == 0 for w, dl in DIL_PAIRS)

    cosf, sinf = _rope_tables(seq)
    xf = x.reshape(batch * seq, d)
    fw = final_norm_w.reshape(1, d)
    ia = ib = 0
    for i in range(depth):
        def ffn(xf, j, final):
            return _ffn(xf, norm_w[i, 2 * j].reshape(1, d), ffn_w_in[i, j].astype(BF16),
                        ffn_w_out[i, j].astype(BF16), fw, final)

        xf = ffn(xf, 0, False)
        nw = norm_w[i, 1].reshape(1, d)
        if i % 2 == 0:
            qkv = _attn_in(xf, nw, attn_w_in[ia].astype(BF16), cosf, sinf, seq)
            outs = [_attn_core(qkv, g, dl, batch, seq) for g, (_, dl) in enumerate(DIL_PAIRS)]
            w_out = attn_w_out[ia].astype(BF16).reshape(len(DIL_PAIRS), gw, d)
            xf = _attn_out(xf, [o for o, _ in outs], [l for _, l in outs], w_out)
            ia += 1
        else:
            w_in = gdn_w_in[ib]
            wba, wat, prm, prmt = _gate_layouts(w_in, gdn_a_log[ib], gdn_dt_bias[ib], conv_dim, vd)
            q, k, v, z, gb, gbt = _gdn_in(xf, nw, w_in.astype(BF16), wba, wat,
                                          gdn_conv_w[ib], prm, prmt, seq, kd, vd)
            on = _gdn_core(q, k, v, z, gb, gbt, gdn_norm_w[ib].reshape(1, LANES), batch, seq)
            xf = _proj_out(xf, on, gdn_w_out[ib].astype(BF16))
            ib += 1
        xf = ffn(xf, 1, i == depth - 1)
    return xf.reshape(batch, seq, d)
```

```python
import functools

import jax
import jax.numpy as jnp
from jax import lax
from jax.experimental import pallas as pl
from jax.experimental.pallas import tpu as pltpu

F32, BF16 = jnp.float32, jnp.bfloat16

RMS_EPS = 1e-6
L2_EPS = 1e-6
LANES = 128
ROPE_THETA = 10000.0
DIL_PAIRS = ((128, 1), (512, 4), (2048, 16))
HEADS_PER_GROUP = 4
SPAN = 128
GDN_K_HEADS = 8
GDN_V_HEADS = 16
GDN_CHUNK = 64
GDN_CONV = 4
GDN_COL_BLOCK = 256
GDN_TIME_BLOCK = 128
TOKEN_TILE = 512
WIDE_TILE = 1024
FFN_COL_BLOCK = 256
ATTN_ROWS = 2048
VMEM_LIMIT_BYTES = 56 * 1024 * 1024
NEG_BIG = -1e30


def _cparams(n_axes):
    return pltpu.CompilerParams(dimension_semantics=("arbitrary",) * n_axes,
                                vmem_limit_bytes=VMEM_LIMIT_BYTES)


def _resident(shape):
    nd = len(shape)
    return pl.BlockSpec(shape, lambda *_: (0,) * nd, pipeline_mode=pl.Buffered(1))


def _resident_of(stacked, lead):
    tail = stacked.shape[len(lead):]
    return pl.BlockSpec((None,) * len(lead) + tail, lambda *_: tuple(lead) + (0,) * len(tail),
                        pipeline_mode=pl.Buffered(1))


def _mm(a, b):
    return jnp.dot(a, b, preferred_element_type=F32)


def _mm_nt(a, b):
    return lax.dot_general(a, b, (((1,), (1,)), ((), ())), preferred_element_type=F32)


def _mm_tn(a, b):
    return lax.dot_general(a, b, (((0,), (0,)), ((), ())), preferred_element_type=F32)


def _rms(x, w):
    ms = jnp.mean(x * x, axis=-1, keepdims=True)
    return x * lax.rsqrt(ms + RMS_EPS) * w


def _silu(x):
    return x * (1.0 / (1.0 + jnp.exp(-x)))


def _softplus(x):
    return jnp.maximum(x, 0.0) + jnp.log1p(jnp.exp(-jnp.abs(x)))


def _ffn_body(x_ref, nw_ref, wi_ref, wo_ref, fw_ref, o_ref, *, final):
    x = x_ref[...]
    h = _rms(x, nw_ref[...]).astype(BF16)
    f = wo_ref.shape[0]
    ff = None
    for c in range(f // FFN_COL_BLOCK):
        fs = slice(c * FFN_COL_BLOCK, (c + 1) * FFN_COL_BLOCK)
        us = slice(f + c * FFN_COL_BLOCK, f + (c + 1) * FFN_COL_BLOCK)
        a = (_silu(_mm(h, wi_ref[:, fs])) * _mm(h, wi_ref[:, us])).astype(BF16)
        part = _mm(a, wo_ref[fs, :])
        ff = part if ff is None else ff + part
    y = x + 0.5 * ff
    if final:
        y = _rms(y, fw_ref[...])
    o_ref[...] = y


def _ffn(x, nw, wi, wo, lead, fw, final):
    n, d = x.shape
    tm = WIDE_TILE
    row = pl.BlockSpec((tm, d), lambda i: (i, 0))
    vec = pl.BlockSpec((1, d), lambda i: (0, 0))
    return pl.pallas_call(
        functools.partial(_ffn_body, final=final),
        grid=(n // tm,),
        in_specs=[row, vec, _resident_of(wi, lead), _resident_of(wo, lead), vec],
        out_specs=row,
        out_shape=jax.ShapeDtypeStruct((n, d), F32),
        compiler_params=_cparams(1),
        name="ffn",
    )(x, nw, wi, wo, fw)


def _attn_in_body(x_ref, nw_ref, w_ref, cos_ref, sin_ref, o_ref, *, n_heads, scale):
    h = _rms(x_ref[...], nw_ref[...]).astype(BF16)
    y = _mm(h, w_ref[...])
    cosf = cos_ref[...]
    sinf = sin_ref[...]
    for hd in range(3 * n_heads):
        blk = y[:, hd * LANES:(hd + 1) * LANES]
        if hd < 2 * n_heads:
            blk = blk * cosf + pltpu.roll(blk, LANES // 2, 1) * sinf
        if hd < n_heads:
            blk = blk * scale
        o_ref[hd] = blk


def _attn_in(x, nw, w, layer, cosf, sinf, seq):
    n, d = x.shape
    c = w.shape[2]
    tm = TOKEN_TILE
    tiles_per_seq = seq // tm
    return pl.pallas_call(
        functools.partial(_attn_in_body, n_heads=c // (3 * LANES), scale=LANES ** -0.5),
        grid=(n // tm,),
        in_specs=[pl.BlockSpec((tm, d), lambda i: (i, 0)),
                  pl.BlockSpec((1, d), lambda i: (0, 0)),
                  _resident_of(w, (layer,)),
                  pl.BlockSpec((tm, LANES), lambda i: (i % tiles_per_seq, 0)),
                  pl.BlockSpec((tm, LANES), lambda i: (i % tiles_per_seq, 0))],
        out_specs=pl.BlockSpec((c // LANES, tm, LANES), lambda i: (0, i, 0)),
        out_shape=jax.ShapeDtypeStruct((c // LANES, n, LANES), F32),
        compiler_params=_cparams(1),
        name="attn_in",
    )(x, nw, w, cosf, sinf)


def _attn_core_body(q_ref, k_ref, v_ref, o_ref, lse_ref, kp_s, vp_s, lse_s, tmp_s, *, dil, nsub):
    n = pl.program_id(1)
    hd = pl.program_id(2)
    win = SPAN * dil
    inner = 4 if dil > 4 else dil

    @pl.when(hd == 0)
    def _():
        lse_s[...] = jnp.zeros(lse_s.shape, F32)

    @pl.when(n == 0)
    def _():
        kp_s[hd] = jnp.zeros(kp_s.shape[1:], BF16)
        vp_s[hd] = jnp.zeros(vp_s.shape[1:], BF16)

    def rows(r, i):
        start = i * win + r
        return slice(start, start + SPAN) if dil == 1 else pl.ds(start, SPAN, stride=dil)

    def gather(t, ref, i):
        if dil <= inner:
            return [ref[0, 0, rows(r, i), :].astype(BF16) for r in range(dil)]
        outer = dil // inner
        for ri in range(inner):
            tmp_s[t, ri] = ref[0, 0, pl.ds(i * win + ri, win // inner, stride=inner), :]
        return [tmp_s[t, r % inner, pl.ds(r // inner, SPAN, stride=outer), :].astype(BF16) for r in range(dil)]

    qi = lax.broadcasted_iota(jnp.int32, (SPAN, SPAN), 0)
    kj = lax.broadcasted_iota(jnp.int32, (SPAN, SPAN), 1)
    mask_c = kj <= qi
    mask_p = kj >= qi
    mask_p0 = jnp.logical_and(mask_p, n > 0)
    lane = lax.broadcasted_iota(jnp.int32, (SPAN, LANES), 1)

    blocks, scs, sps, vcs, vps = [], [], [], [], []
    kprev = [kp_s[hd, r] for r in range(dil)]
    vprev = [vp_s[hd, r] for r in range(dil)]
    for i in range(nsub):
        qs, kcs, vws = gather(0, q_ref, i), gather(1, k_ref, i), gather(2, v_ref, i)
        for r in range(dil):
            blocks.append((r, i))
            scs.append(jnp.where(mask_c, _mm_nt(qs[r], kcs[r]), NEG_BIG))
            sps.append(jnp.where(mask_p0 if i == 0 else mask_p, _mm_nt(qs[r], kprev[r]), NEG_BIG))
            vcs.append(vws[r])
            vps.append(vprev[r])
        kprev, vprev = kcs, vws
    for r in range(dil):
        kp_s[hd, r] = kprev[r]
        vp_s[hd, r] = vprev[r]
    pcs, pps, dens, lses = [], [], [], []
    for sc, sp in zip(scs, sps):
        m = jnp.maximum(jnp.max(sc, axis=-1, keepdims=True), jnp.max(sp, axis=-1, keepdims=True))
        pc = jnp.exp(sc - m)
        pp = jnp.exp(sp - m)
        den = jnp.sum(pc, axis=-1, keepdims=True) + jnp.sum(pp, axis=-1, keepdims=True)
        pcs.append(pc.astype(BF16))
        pps.append(pp.astype(BF16))
        dens.append(den)
        lses.append(m + jnp.log(den))
    for b, (r, i) in enumerate(blocks):
        o_ref[0, 0, rows(r, i), :] = (_mm(pcs[b], vcs[b]) + _mm(pps[b], vps[b])) / dens[b]
        lse_s[b] = jnp.where(lane == hd, lses[b], lse_s[b])

    @pl.when(hd == pl.num_programs(2) - 1)
    def _():
        for b, (r, i) in enumerate(blocks):
            lse_ref[0, rows(r, i), :] = lse_s[b]


def _attn_core(qkv, group, dil, batch, seq):
    nh = qkv.shape[0] // 3
    hpg = HEADS_PER_GROUP
    nsub = max(1, ATTN_ROWS // (SPAN * dil))
    rows = nsub * SPAN * dil
    qkv4 = qkv.reshape(3 * nh, batch, seq, LANES)

    def spec(which):
        return pl.BlockSpec((1, 1, rows, LANES), lambda b, n, h: (which * nh + group * hpg + h, b, n, 0))

    o, lse = pl.pallas_call(
        functools.partial(_attn_core_body, dil=dil, nsub=nsub),
        grid=(batch, seq // rows, hpg),
        in_specs=[spec(0), spec(1), spec(2)],
        out_specs=[pl.BlockSpec((1, 1, rows, LANES), lambda b, n, h: (h, b, n, 0)),
                   pl.BlockSpec((1, rows, LANES), lambda b, n, h: (b, n, 0))],
        out_shape=[jax.ShapeDtypeStruct((hpg, batch, seq, LANES), F32),
                   jax.ShapeDtypeStruct((batch, seq, LANES), F32)],
        scratch_shapes=[pltpu.VMEM((hpg, dil, SPAN, LANES), BF16), pltpu.VMEM((hpg, dil, SPAN, LANES), BF16),
                        pltpu.VMEM((nsub * dil, SPAN, LANES), F32),
                        pltpu.VMEM((3, 4, SPAN * dil // 4, LANES), F32)],
        compiler_params=_cparams(3),
        name=f"attn_core_d{dil}",
    )(qkv4, qkv4, qkv4)
    return o.reshape(hpg, batch * seq, LANES), lse.reshape(batch * seq, LANES)


def _attn_out_body(x_ref, o0_ref, o1_ref, o2_ref, l0_ref, l1_ref, l2_ref, w_ref, out_ref):
    ls = [l0_ref[...], l1_ref[...], l2_ref[...]]
    m = jnp.maximum(jnp.maximum(ls[0], ls[1]), ls[2])
    es = [jnp.exp(l - m) for l in ls]
    inv = 1.0 / (es[0] + es[1] + es[2])
    acc = x_ref[...]
    for g, o_ref in enumerate((o0_ref, o1_ref, o2_ref)):
        alpha = es[g] * inv
        scaled = jnp.concatenate([o_ref[hg] * alpha[:, hg:hg + 1] for hg in range(HEADS_PER_GROUP)], axis=1)
        acc = acc + _mm(scaled.astype(BF16), w_ref[g])
    out_ref[...] = acc


def _attn_out(x, os_, lses, w, layer):
    n, d = x.shape
    tm = WIDE_TILE
    row = lambda width: pl.BlockSpec((tm, width), lambda i: (i, 0))
    heads = pl.BlockSpec((HEADS_PER_GROUP, tm, LANES), lambda i: (0, i, 0))
    return pl.pallas_call(
        _attn_out_body,
        grid=(n // tm,),
        in_specs=[row(d)] + [heads] * 3 + [row(LANES)] * 3 + [_resident_of(w, (layer,))],
        out_specs=row(d),
        out_shape=jax.ShapeDtypeStruct((n, d), F32),
        compiler_params=_cparams(1),
        name="attn_out",
    )(x, *os_, *lses, w)


def _segmented_cumsum(x, axis):
    pos = lax.broadcasted_iota(jnp.int32, x.shape, axis) % GDN_CHUNK
    s = 1
    while s < GDN_CHUNK:
        x = x + jnp.where(pos >= s, pltpu.roll(x, s, axis), 0.0)
        s *= 2
    return x


def _gdn_in_body(x_ref, nw_ref, w_ref, wba_ref, wat_ref, cw_ref, prm_ref, prmt_ref,
                 q_ref, k_ref, v_ref, z_ref, gb_ref, gbt_ref, pre_s, halo_s, stage_s, *, tiles_per_seq, kd, vd):
    i = pl.program_id(0)
    tm = x_ref.shape[0]
    halo = 8
    conv_dim = 2 * kd + vd
    cbw = GDN_COL_BLOCK
    h = _rms(x_ref[...], nw_ref[...]).astype(BF16)

    @pl.when(i % tiles_per_seq == 0)
    def _():
        halo_s[...] = jnp.zeros(halo_s.shape, F32)

    def conv_block(cb):
        buf = cb % 2
        hrows = tm // 2
        for hh in range(cbw // LANES):
            c0 = cb * cbw + hh * LANES
            sl = slice(c0, c0 + LANES)
            taps = [cw_ref[j:j + 1, sl] for j in range(GDN_CONV)]
            first = halo - (GDN_CONV - 1)
            wins = [pre_s[buf, hh, pl.ds(first + k, hrows, stride=2), :] for k in range(GDN_CONV + 1)]
            for par in range(2):
                acc = wins[par] * taps[0]
                for j in range(1, GDN_CONV):
                    acc = acc + wins[par + j] * taps[j]
                a = _silu(acc)
                if c0 < 2 * kd:
                    r = lax.rsqrt(jnp.sum(a * a, axis=-1, keepdims=True) + L2_EPS)
                    a = a * (r * (LANES ** -0.5) if c0 < kd else r)
                stage_s[hh, pl.ds(par, hrows, stride=2), :] = a
            a = stage_s[hh].astype(BF16)
            if c0 < kd:
                q_ref[:, sl] = a
            elif c0 < 2 * kd:
                k_ref[:, c0 - kd:c0 - kd + LANES] = a
            else:
                v_ref[:, c0 - 2 * kd:c0 - 2 * kd + LANES] = a

    nvh = GDN_V_HEADS
    ba = _mm(h, wba_ref[...])
    lane = lax.broadcasted_iota(jnp.int32, ba.shape, 1)
    beta = 1.0 / (1.0 + jnp.exp(-ba))
    g = -jnp.exp(prm_ref[0:1, :]) * _softplus(ba + prm_ref[1:2, :])
    gb_ref[...] = jnp.where(lane < nvh, beta, _segmented_cumsum(g, 0))
    at = _mm_nt(wat_ref[...], h)
    cpl = LANES // GDN_CHUNK
    for p in range(tm // LANES):
        blk = at[:, p * LANES:(p + 1) * LANES]
        cum = _segmented_cumsum(-jnp.exp(prmt_ref[0]) * _softplus(blk + prmt_ref[1]), 1)
        ev, od = cum[:nvh // 2], cum[nvh // 2:]
        left = lax.broadcasted_iota(jnp.int32, ev.shape, 1) < GDN_CHUNK
        gbt_ref[cpl * p] = jnp.where(left, ev, pltpu.roll(od, GDN_CHUNK, 1))
        gbt_ref[cpl * p + 1] = jnp.where(left, pltpu.roll(ev, GDN_CHUNK, 1), od)

    n_conv = conv_dim // cbw
    n_z = vd // cbw
    for cb in range(n_conv + 1):
        if cb < n_conv:
            cs_ = slice(cb * cbw, (cb + 1) * cbw)
            yb = _mm(h, w_ref[:, cs_])
            buf = cb % 2
            for hh in range(cbw // LANES):
                hs = slice(cb * cbw + hh * LANES, cb * cbw + (hh + 1) * LANES)
                pre_s[buf, hh, 0:halo, :] = halo_s[:, hs]
                pre_s[buf, hh, halo:halo + tm, :] = yb[:, hh * LANES:(hh + 1) * LANES]
            halo_s[:, cs_] = yb[tm - halo:, :]
        if cb > 0 and cb % (n_conv // n_z) == 0:
            zb = cb // (n_conv // n_z) - 1
            zs = slice(zb * cbw, (zb + 1) * cbw)
            z_ref[:, zs] = _mm(h, w_ref[:, conv_dim + zb * cbw:conv_dim + (zb + 1) * cbw]).astype(BF16)
        if cb > 0:
            conv_block(cb - 1)


def _gdn_in(x, nw, w, layer, wba, wat, cw, prm, prmt, seq, kd, vd):
    n, d = x.shape
    tm = TOKEN_TILE
    conv_dim = 2 * kd + vd
    row = lambda width: pl.BlockSpec((tm, width), lambda i: (i, 0))
    return pl.pallas_call(
        functools.partial(_gdn_in_body, tiles_per_seq=seq // tm, kd=kd, vd=vd),
        grid=(n // tm,),
        in_specs=[row(d), pl.BlockSpec((1, d), lambda i: (0, 0)), _resident_of(w, (layer,)), _resident(wba.shape),
                  _resident(wat.shape), _resident_of(cw, (layer,)), _resident(prm.shape), _resident(prmt.shape)],
        out_specs=[row(kd), row(kd), row(vd), row(vd), row(LANES),
                   pl.BlockSpec((tm // GDN_CHUNK, GDN_K_HEADS, LANES), lambda i: (i, 0, 0))],
        out_shape=[jax.ShapeDtypeStruct((n, kd), BF16), jax.ShapeDtypeStruct((n, kd), BF16),
                   jax.ShapeDtypeStruct((n, vd), BF16), jax.ShapeDtypeStruct((n, vd), BF16),
                   jax.ShapeDtypeStruct((n, LANES), F32),
                   jax.ShapeDtypeStruct((n // GDN_CHUNK, GDN_K_HEADS, LANES), F32)],
        scratch_shapes=[pltpu.VMEM((2, GDN_COL_BLOCK // LANES, tm + 8, LANES), F32),
                        pltpu.VMEM((8, conv_dim), F32),
                        pltpu.VMEM((GDN_COL_BLOCK // LANES, tm, LANES), F32)],
        compiler_params=_cparams(1),
        name="gdn_in",
    )(x, nw, w, wba, wat, cw, prm, prmt)


def _gdn_core_body(q_ref, k_ref, v_ref, z_ref, gb_ref, gbt_ref, nw_ref, o_ref, st_s):
    cs = GDN_CHUNK
    nchunks = q_ref.shape[1] // cs
    npairs = GDN_K_HEADS
    gofs = GDN_V_HEADS

    @pl.when(pl.program_id(1) == 0)
    def _():
        st_s[...] = jnp.zeros(st_s.shape, F32)

    row = lax.broadcasted_iota(jnp.int32, (cs, LANES), 0)
    lane = lax.broadcasted_iota(jnp.int32, (cs, LANES), 1)
    colp = lane % cs
    left = lane < cs
    incl = row >= colp
    strict = row > colp
    eye2 = jnp.where(row == colp, 1.0, 0.0).astype(F32)
    nw = nw_ref[...]

    def blockdiag(x):
        return jnp.concatenate([jnp.where(left, x, 0.0), jnp.where(left, 0.0, x)], axis=0).astype(BF16)

    def side(x, s):
        return x[:, :cs] if s == 0 else pltpu.roll(x, cs, 1)[:, :cs]

    pairs = [(c, hk) for c in range(nchunks) for hk in range(npairs)]
    xs, ps, qkds = [], [], []
    for c, hk in pairs:
        rs = slice(c * cs, (c + 1) * cs)
        ks = slice(hk * LANES, (hk + 1) * LANES)
        gbc = gb_ref[0, rs, :]
        qb = q_ref[0, rs, ks]
        kb = k_ref[0, rs, ks]
        kq = _mm_nt(jnp.concatenate([qb, kb], axis=0), jnp.concatenate([kb, kb], axis=0))
        ja, jb = 2 * hk, 2 * hk + 1
        beta2 = jnp.where(left, gbc[:, ja:ja + 1], gbc[:, jb:jb + 1])
        gc2 = jnp.where(left, gbc[:, gofs + ja:gofs + ja + 1], gbc[:, gofs + jb:gofs + jb + 1])
        dec2 = jnp.where(incl, jnp.exp(jnp.where(incl, gc2 - gbt_ref[c, hk:hk + 1, :], 0.0)), 0.0)
        x = jnp.where(strict, -(beta2 * kq[cs:] * dec2), 0.0)
        xs.append(x)
        ps.append(eye2 + x)
        qkds.append(kq[:cs] * dec2)

    bds = [blockdiag(x) for x in xs]
    for _ in range(5):
        xs = [_mm(x.astype(BF16), bd) for x, bd in zip(xs, bds)]
        bds = [blockdiag(x) for x in xs]
        ps = [p + _mm(p.astype(BF16), bd) for p, bd in zip(ps, bds)]

    us, wqs, kdecs, cdecs, qkdh = {}, {}, {}, {}, {}
    for i, (c, hk) in enumerate(pairs):
        rs = slice(c * cs, (c + 1) * cs)
        ks = slice(hk * LANES, (hk + 1) * LANES)
        gbc = gb_ref[0, rs, :]
        qf = q_ref[0, rs, ks].astype(F32)
        kf = k_ref[0, rs, ks].astype(F32)
        for s in range(2):
            j = 2 * hk + s
            vs = slice(j * LANES, (j + 1) * LANES)
            beta = gbc[:, j:j + 1]
            gc = gbc[:, gofs + j:gofs + j + 1]
            eg = jnp.exp(gc)
            gl = gc[cs - 1:cs, :]
            rhs = jnp.concatenate([v_ref[0, rs, vs].astype(F32) * beta, kf * (beta * eg)], axis=1).astype(BF16)
            sol = _mm(side(ps[i], s).astype(BF16), rhs)
            us[c, j] = sol[:, :LANES]
            wqs[c, j] = jnp.concatenate([sol[:, LANES:], qf * eg], axis=0).astype(BF16)
            kdecs[c, j] = (kf * jnp.exp(gl - gc)).astype(BF16)
            cdecs[c, j] = jnp.exp(gl)
            qkdh[c, j] = side(qkds[i], s).astype(BF16)

    for c in range(nchunks):
        rs = slice(c * cs, (c + 1) * cs)
        states = [st_s[j] for j in range(GDN_V_HEADS)]
        boths = [_mm(wqs[c, j], states[j].astype(BF16)) for j in range(GDN_V_HEADS)]
        for j in range(GDN_V_HEADS):
            vs = slice(j * LANES, (j + 1) * LANES)
            vb = (us[c, j] - boths[j][:cs]).astype(BF16)
            o = boths[j][cs:] + _mm(qkdh[c, j], vb)
            st_s[j] = states[j] * cdecs[c, j] + _mm_tn(kdecs[c, j], vb)
            on = o * lax.rsqrt(jnp.mean(o * o, axis=-1, keepdims=True) + RMS_EPS) * nw
            o_ref[0, rs, vs] = (on * _silu(z_ref[0, rs, vs].astype(F32))).astype(BF16)


def _gdn_core(q, k, v, z, gb, gbt, nw, batch, seq):
    n, kd = q.shape
    vd = v.shape[1]
    tb = GDN_TIME_BLOCK
    nt = seq // tb
    tok = lambda width: pl.BlockSpec((1, tb, width), lambda b, t: (b, t, 0))
    o = pl.pallas_call(
        _gdn_core_body,
        grid=(batch, nt),
        in_specs=[tok(kd), tok(kd), tok(vd), tok(vd), tok(LANES),
                  pl.BlockSpec((tb // GDN_CHUNK, GDN_K_HEADS, LANES), lambda b, t: (b * nt + t, 0, 0)),
                  pl.BlockSpec((1, LANES), lambda b, t: (0, 0))],
        out_specs=tok(vd),
        out_shape=jax.ShapeDtypeStruct((batch, seq, vd), BF16),
        scratch_shapes=[pltpu.VMEM((GDN_V_HEADS, LANES, LANES), F32)],
        compiler_params=_cparams(2),
        name="gdn_core",
    )(q.reshape(batch, seq, kd), k.reshape(batch, seq, kd), v.reshape(batch, seq, vd), z.reshape(batch, seq, vd),
      gb.reshape(batch, seq, LANES), gbt, nw)
    return o.reshape(n, vd)


def _proj_out_body(x_ref, a_ref, w_ref, o_ref):
    o_ref[...] = x_ref[...] + _mm(a_ref[...], w_ref[...])


def _proj_out(x, a, w, layer):
    n, d = x.shape
    tm = WIDE_TILE
    return pl.pallas_call(
        _proj_out_body,
        grid=(n // tm,),
        in_specs=[pl.BlockSpec((tm, d), lambda i: (i, 0)), pl.BlockSpec((tm, a.shape[1]), lambda i: (i, 0)),
                  _resident_of(w, (layer,))],
        out_specs=pl.BlockSpec((tm, d), lambda i: (i, 0)),
        out_shape=jax.ShapeDtypeStruct((n, d), F32),
        compiler_params=_cparams(1),
        name="gdn_out",
    )(x, a, w)


def _rope_tables(seq):
    inv_freq = 1.0 / (ROPE_THETA ** (jnp.arange(0, LANES, 2, dtype=F32) / LANES))
    ang = jnp.arange(seq, dtype=F32)[:, None] * inv_freq[None, :]
    cos, sin = jnp.cos(ang), jnp.sin(ang)
    return jnp.concatenate([cos, cos], axis=-1), jnp.concatenate([-sin, sin], axis=-1)


def _gate_layouts(w_in, a_log, dt_bias, conv_dim, vd):
    nvh = GDN_V_HEADS
    wb = w_in[:, conv_dim + vd:conv_dim + vd + nvh]
    wa = w_in[:, conv_dim + vd + nvh:]
    wba = jnp.pad(jnp.concatenate([wb, wa], axis=1), ((0, 0), (0, LANES - 2 * nvh))).astype(BF16)
    wat = jnp.concatenate([wa[:, 0::2], wa[:, 1::2]], axis=1).T.astype(BF16)
    pad = lambda p: jnp.pad(p.reshape(1, nvh), ((0, 0), (nvh, LANES - 2 * nvh)))
    prm = jnp.pad(jnp.concatenate([pad(a_log), pad(dt_bias)], axis=0), ((0, 6), (0, 0)))
    eo = lambda p: jnp.broadcast_to(jnp.concatenate([p[0::2], p[1::2]]).reshape(nvh, 1), (nvh, LANES))
    prmt = jnp.stack([eo(a_log), eo(dt_bias)])
    return wba, wat, prm, prmt


def kernel(x, norm_w, ffn_w_in, ffn_w_out, attn_w_in, attn_w_out, gdn_w_in, gdn_conv_w, gdn_a_log,
           gdn_dt_bias, gdn_norm_w, gdn_w_out, final_norm_w):
    batch, seq, d = x.shape
    depth = norm_w.shape[0]
    kd = GDN_K_HEADS * LANES
    vd = GDN_V_HEADS * LANES
    conv_dim = 2 * kd + vd
    gw = HEADS_PER_GROUP * LANES
    assert seq % TOKEN_TILE == 0 and seq % GDN_TIME_BLOCK == 0
    assert all(w // dl == SPAN and seq % (dl * SPAN) == 0 for w, dl in DIL_PAIRS)

    cosf, sinf = _rope_tables(seq)
    ffn_wi, ffn_wo = ffn_w_in.astype(BF16), ffn_w_out.astype(BF16)
    attn_wi = attn_w_in.astype(BF16)
    attn_wo = attn_w_out.astype(BF16).reshape(attn_w_out.shape[0], len(DIL_PAIRS), gw, d)
    gdn_wi, gdn_wo = gdn_w_in.astype(BF16), gdn_w_out.astype(BF16)
    xf = x.reshape(batch * seq, d)
    fw = final_norm_w.reshape(1, d)
    ia = ib = 0
    for i in range(depth):
        def ffn(xf, j, final):
            return _ffn(xf, norm_w[i, 2 * j].reshape(1, d), ffn_wi, ffn_wo, (i, j), fw, final)

        xf = ffn(xf, 0, False)
        nw = norm_w[i, 1].reshape(1, d)
        if i % 2 == 0:
            qkv = _attn_in(xf, nw, attn_wi, ia, cosf, sinf, seq)
            outs = [_attn_core(qkv, g, dl, batch, seq) for g, (_, dl) in enumerate(DIL_PAIRS)]
            xf = _attn_out(xf, [o for o, _ in outs], [l for _, l in outs], attn_wo, ia)
            ia += 1
        else:
            w_in = gdn_w_in[ib]
            wba, wat, prm, prmt = _gate_layouts(w_in, gdn_a_log[ib], gdn_dt_bias[ib], conv_dim, vd)
            q, k, v, z, gb, gbt = _gdn_in(xf, nw, gdn_wi, ib, wba, wat, gdn_conv_w, prm, prmt, seq, kd, vd)
            on = _gdn_core(q, k, v, z, gb, gbt, gdn_norm_w[ib].reshape(1, LANES), batch, seq)
            xf = _proj_out(xf, on, gdn_wo, ib)
            ib += 1
        xf = ffn(xf, 1, i == depth - 1)
    return xf.reshape(batch, seq, d)
```

```python
import functools

import jax
import jax.numpy as jnp
from jax import lax
from jax.experimental import pallas as pl
from jax.experimental.pallas import tpu as pltpu

F32, BF16 = jnp.float32, jnp.bfloat16

RMS_EPS = 1e-6
L2_EPS = 1e-6
LANES = 128
ROPE_THETA = 10000.0
DIL_PAIRS = ((128, 1), (512, 4), (2048, 16))
HEADS_PER_GROUP = 4
SPAN = 128
GDN_K_HEADS = 8
GDN_V_HEADS = 16
GDN_CHUNK = 64
GDN_CONV = 4
GDN_COL_BLOCK = 256
GDN_TIME_BLOCK = 128
TOKEN_TILE = 512
WIDE_TILE = 1024
FFN_COL_BLOCK = 256
ATTN_ROWS = 2048
VMEM_LIMIT_BYTES = 56 * 1024 * 1024
NEG_BIG = -1e30


def _cparams(n_axes):
    return pltpu.CompilerParams(dimension_semantics=("arbitrary",) * n_axes,
                                vmem_limit_bytes=VMEM_LIMIT_BYTES)


def _resident(shape):
    nd = len(shape)
    return pl.BlockSpec(shape, lambda *_: (0,) * nd, pipeline_mode=pl.Buffered(1))


def _resident_of(stacked, lead):
    tail = stacked.shape[len(lead):]
    return pl.BlockSpec((None,) * len(lead) + tail, lambda *_: tuple(lead) + (0,) * len(tail),
                        pipeline_mode=pl.Buffered(1))


def _mm(a, b):
    return jnp.dot(a, b, preferred_element_type=F32)


def _mm_nt(a, b):
    return lax.dot_general(a, b, (((1,), (1,)), ((), ())), preferred_element_type=F32)


def _mm_tn(a, b):
    return lax.dot_general(a, b, (((0,), (0,)), ((), ())), preferred_element_type=F32)


def _rms(x, w):
    ms = jnp.mean(x * x, axis=-1, keepdims=True)
    return x * lax.rsqrt(ms + RMS_EPS) * w


def _silu(x):
    return x * (1.0 / (1.0 + jnp.exp(-x)))


def _softplus(x):
    return jnp.maximum(x, 0.0) + jnp.log1p(jnp.exp(-jnp.abs(x)))


def _ffn_body(x_ref, nw_ref, wi_ref, wo_ref, fw_ref, o_ref, *, final):
    x = x_ref[...]
    h = _rms(x, nw_ref[...]).astype(BF16)
    f = wo_ref.shape[0]
    ff = None
    for c in range(f // FFN_COL_BLOCK):
        fs = slice(c * FFN_COL_BLOCK, (c + 1) * FFN_COL_BLOCK)
        us = slice(f + c * FFN_COL_BLOCK, f + (c + 1) * FFN_COL_BLOCK)
        a = (_silu(_mm(h, wi_ref[:, fs])) * _mm(h, wi_ref[:, us])).astype(BF16)
        part = _mm(a, wo_ref[fs, :])
        ff = part if ff is None else ff + part
    y = x + 0.5 * ff
    if final:
        y = _rms(y, fw_ref[...])
    o_ref[...] = y


def _ffn(x, nw, wi, wo, lead, fw, final):
    n, d = x.shape
    tm = WIDE_TILE
    row = pl.BlockSpec((tm, d), lambda i: (i, 0))
    vec = pl.BlockSpec((1, d), lambda i: (0, 0))
    return pl.pallas_call(
        functools.partial(_ffn_body, final=final),
        grid=(n // tm,),
        in_specs=[row, vec, _resident_of(wi, lead), _resident_of(wo, lead), vec],
        out_specs=row,
        out_shape=jax.ShapeDtypeStruct((n, d), F32),
        compiler_params=_cparams(1),
        name="ffn",
    )(x, nw, wi, wo, fw)


def _attn_in_body(x_ref, nw_ref, w_ref, cos_ref, sin_ref, o_ref, *, n_heads, scale):
    h = _rms(x_ref[...], nw_ref[...]).astype(BF16)
    y = _mm(h, w_ref[...])
    cosf = cos_ref[...]
    sinf = sin_ref[...]
    for hd in range(3 * n_heads):
        blk = y[:, hd * LANES:(hd + 1) * LANES]
        if hd < 2 * n_heads:
            blk = blk * cosf + pltpu.roll(blk, LANES // 2, 1) * sinf
        if hd < n_heads:
            blk = blk * scale
        o_ref[hd] = blk


def _attn_in(x, nw, w, layer, cosf, sinf, seq):
    n, d = x.shape
    c = w.shape[2]
    tm = TOKEN_TILE
    tiles_per_seq = seq // tm
    return pl.pallas_call(
        functools.partial(_attn_in_body, n_heads=c // (3 * LANES), scale=LANES ** -0.5),
        grid=(n // tm,),
        in_specs=[pl.BlockSpec((tm, d), lambda i: (i, 0)),
                  pl.BlockSpec((1, d), lambda i: (0, 0)),
                  _resident_of(w, (layer,)),
                  pl.BlockSpec((tm, LANES), lambda i: (i % tiles_per_seq, 0)),
                  pl.BlockSpec((tm, LANES), lambda i: (i % tiles_per_seq, 0))],
        out_specs=pl.BlockSpec((c // LANES, tm, LANES), lambda i: (0, i, 0)),
        out_shape=jax.ShapeDtypeStruct((c // LANES, n, LANES), F32),
        compiler_params=_cparams(1),
        name="attn_in",
    )(x, nw, w, cosf, sinf)


def _attn_core_body(q_ref, k_ref, v_ref, o_ref, lse_ref, kp_s, vp_s, lse_s, tmp_s, *, dil, nsub):
    n = pl.program_id(1)
    hd = pl.program_id(2)
    win = SPAN * dil
    inner = 4 if dil > 4 else dil

    @pl.when(hd == 0)
    def _():
        lse_s[...] = jnp.zeros(lse_s.shape, F32)

    @pl.when(n == 0)
    def _():
        kp_s[hd] = jnp.zeros(kp_s.shape[1:], BF16)
        vp_s[hd] = jnp.zeros(vp_s.shape[1:], BF16)

    def rows(r, i):
        start = i * win + r
        return slice(start, start + SPAN) if dil == 1 else pl.ds(start, SPAN, stride=dil)

    def gather(t, ref, i):
        if dil <= inner:
            return [ref[0, 0, rows(r, i), :].astype(BF16) for r in range(dil)]
        outer = dil // inner
        for ri in range(inner):
            tmp_s[t, ri] = ref[0, 0, pl.ds(i * win + ri, win // inner, stride=inner), :]
        return [tmp_s[t, r % inner, pl.ds(r // inner, SPAN, stride=outer), :].astype(BF16) for r in range(dil)]

    qi = lax.broadcasted_iota(jnp.int32, (SPAN, SPAN), 0)
    kj = lax.broadcasted_iota(jnp.int32, (SPAN, SPAN), 1)
    mask_c = kj <= qi
    mask_p = kj >= qi
    mask_p0 = jnp.logical_and(mask_p, n > 0)
    lane = lax.broadcasted_iota(jnp.int32, (SPAN, LANES), 1)

    blocks, scs, sps, vcs, vps = [], [], [], [], []
    kprev = [kp_s[hd, r] for r in range(dil)]
    vprev = [vp_s[hd, r] for r in range(dil)]
    for i in range(nsub):
        qs, kcs, vws = gather(0, q_ref, i), gather(1, k_ref, i), gather(2, v_ref, i)
        for r in range(dil):
            blocks.append((r, i))
            scs.append(jnp.where(mask_c, _mm_nt(qs[r], kcs[r]), NEG_BIG))
            sps.append(jnp.where(mask_p0 if i == 0 else mask_p, _mm_nt(qs[r], kprev[r]), NEG_BIG))
            vcs.append(vws[r])
            vps.append(vprev[r])
        kprev, vprev = kcs, vws
    for r in range(dil):
        kp_s[hd, r] = kprev[r]
        vp_s[hd, r] = vprev[r]
    pcs, pps, dens, lses = [], [], [], []
    for sc, sp in zip(scs, sps):
        m = jnp.maximum(jnp.max(sc, axis=-1, keepdims=True), jnp.max(sp, axis=-1, keepdims=True))
        pc = jnp.exp(sc - m)
        pp = jnp.exp(sp - m)
        den = jnp.sum(pc, axis=-1, keepdims=True) + jnp.sum(pp, axis=-1, keepdims=True)
        pcs.append(pc.astype(BF16))
        pps.append(pp.astype(BF16))
        dens.append(den)
        lses.append(m + jnp.log(den))
    for b, (r, i) in enumerate(blocks):
        o_ref[0, 0, rows(r, i), :] = (_mm(pcs[b], vcs[b]) + _mm(pps[b], vps[b])) / dens[b]
        lse_s[b] = jnp.where(lane == hd, lses[b], lse_s[b])

    @pl.when(hd == pl.num_programs(2) - 1)
    def _():
        for b, (r, i) in enumerate(blocks):
            lse_ref[0, rows(r, i), :] = lse_s[b]


def _attn_core(qkv, group, dil, batch, seq):
    nh = qkv.shape[0] // 3
    hpg = HEADS_PER_GROUP
    nsub = max(1, ATTN_ROWS // (SPAN * dil))
    rows = nsub * SPAN * dil
    qkv4 = qkv.reshape(3 * nh, batch, seq, LANES)

    def spec(which):
        return pl.BlockSpec((1, 1, rows, LANES), lambda b, n, h: (which * nh + group * hpg + h, b, n, 0))

    o, lse = pl.pallas_call(
        functools.partial(_attn_core_body, dil=dil, nsub=nsub),
        grid=(batch, seq // rows, hpg),
        in_specs=[spec(0), spec(1), spec(2)],
        out_specs=[pl.BlockSpec((1, 1, rows, LANES), lambda b, n, h: (h, b, n, 0)),
                   pl.BlockSpec((1, rows, LANES), lambda b, n, h: (b, n, 0))],
        out_shape=[jax.ShapeDtypeStruct((hpg, batch, seq, LANES), F32),
                   jax.ShapeDtypeStruct((batch, seq, LANES), F32)],
        scratch_shapes=[pltpu.VMEM((hpg, dil, SPAN, LANES), BF16), pltpu.VMEM((hpg, dil, SPAN, LANES), BF16),
                        pltpu.VMEM((nsub * dil, SPAN, LANES), F32),
                        pltpu.VMEM((3, 4, SPAN * dil // 4, LANES), F32)],
        compiler_params=_cparams(3),
        name=f"attn_core_d{dil}",
    )(qkv4, qkv4, qkv4)
    return o.reshape(hpg, batch * seq, LANES), lse.reshape(batch * seq, LANES)


def _attn_out_body(x_ref, o0_ref, o1_ref, o2_ref, l0_ref, l1_ref, l2_ref, w_ref, out_ref):
    ls = [l0_ref[...], l1_ref[...], l2_ref[...]]
    m = jnp.maximum(jnp.maximum(ls[0], ls[1]), ls[2])
    es = [jnp.exp(l - m) for l in ls]
    inv = 1.0 / (es[0] + es[1] + es[2])
    acc = x_ref[...]
    for g, o_ref in enumerate((o0_ref, o1_ref, o2_ref)):
        alpha = es[g] * inv
        scaled = jnp.concatenate([o_ref[hg] * alpha[:, hg:hg + 1] for hg in range(HEADS_PER_GROUP)], axis=1)
        acc = acc + _mm(scaled.astype(BF16), w_ref[g])
    out_ref[...] = acc


def _attn_out(x, os_, lses, w, layer):
    n, d = x.shape
    tm = WIDE_TILE
    row = lambda width: pl.BlockSpec((tm, width), lambda i: (i, 0))
    heads = pl.BlockSpec((HEADS_PER_GROUP, tm, LANES), lambda i: (0, i, 0))
    return pl.pallas_call(
        _attn_out_body,
        grid=(n // tm,),
        in_specs=[row(d)] + [heads] * 3 + [row(LANES)] * 3 + [_resident_of(w, (layer,))],
        out_specs=row(d),
        out_shape=jax.ShapeDtypeStruct((n, d), F32),
        compiler_params=_cparams(1),
        name="attn_out",
    )(x, *os_, *lses, w)


def _segmented_cumsum(x, axis):
    pos = lax.broadcasted_iota(jnp.int32, x.shape, axis) % GDN_CHUNK
    s = 1
    while s < GDN_CHUNK:
        x = x + jnp.where(pos >= s, pltpu.roll(x, s, axis), 0.0)
        s *= 2
    return x


def _gdn_in_body(x_ref, nw_ref, w_ref, wba_ref, wat_ref, cw_ref, prm_ref, prmt_ref,
                 q_ref, k_ref, v_ref, z_ref, gb_ref, gbt_ref, pre_s, halo_s, stage_s, *, tiles_per_seq, kd, vd):
    i = pl.program_id(0)
    tm = x_ref.shape[0]
    halo = 8
    conv_dim = 2 * kd + vd
    cbw = GDN_COL_BLOCK
    h = _rms(x_ref[...], nw_ref[...]).astype(BF16)

    @pl.when(i % tiles_per_seq == 0)
    def _():
        halo_s[...] = jnp.zeros(halo_s.shape, F32)

    def conv_block(cb):
        buf = cb % 2
        hrows = tm // 2
        for hh in range(cbw // LANES):
            c0 = cb * cbw + hh * LANES
            sl = slice(c0, c0 + LANES)
            taps = [cw_ref[j:j + 1, sl] for j in range(GDN_CONV)]
            first = halo - (GDN_CONV - 1)
            wins = [pre_s[buf, hh, pl.ds(first + k, hrows, stride=2), :] for k in range(GDN_CONV + 1)]
            for par in range(2):
                acc = wins[par] * taps[0]
                for j in range(1, GDN_CONV):
                    acc = acc + wins[par + j] * taps[j]
                a = _silu(acc)
                if c0 < 2 * kd:
                    r = lax.rsqrt(jnp.sum(a * a, axis=-1, keepdims=True) + L2_EPS)
                    a = a * (r * (LANES ** -0.5) if c0 < kd else r)
                stage_s[hh, pl.ds(par, hrows, stride=2), :] = a
            a = stage_s[hh].astype(BF16)
            if c0 < kd:
                q_ref[:, sl] = a
            elif c0 < 2 * kd:
                k_ref[:, c0 - kd:c0 - kd + LANES] = a
            else:
                v_ref[:, c0 - 2 * kd:c0 - 2 * kd + LANES] = a

    nvh = GDN_V_HEADS
    ba = _mm(h, wba_ref[...])
    lane = lax.broadcasted_iota(jnp.int32, ba.shape, 1)
    beta = 1.0 / (1.0 + jnp.exp(-ba))
    g = -jnp.exp(prm_ref[0:1, :]) * _softplus(ba + prm_ref[1:2, :])
    gb_ref[...] = jnp.where(lane < nvh, beta, _segmented_cumsum(g, 0))
    at = _mm_nt(wat_ref[...], h)
    cpl = LANES // GDN_CHUNK
    for p in range(tm // LANES):
        blk = at[:, p * LANES:(p + 1) * LANES]
        cum = _segmented_cumsum(-jnp.exp(prmt_ref[0]) * _softplus(blk[:nvh] + prmt_ref[1]), 1)
        bet = 1.0 / (1.0 + jnp.exp(-blk[nvh:]))
        left = lax.broadcasted_iota(jnp.int32, (nvh // 2, LANES), 1) < GDN_CHUNK
        for t, val in enumerate((cum, bet)):
            ev, od = val[:nvh // 2], val[nvh // 2:]
            rs = slice(t * nvh // 2, (t + 1) * nvh // 2)
            gbt_ref[cpl * p, rs, :] = jnp.where(left, ev, pltpu.roll(od, GDN_CHUNK, 1))
            gbt_ref[cpl * p + 1, rs, :] = jnp.where(left, pltpu.roll(ev, GDN_CHUNK, 1), od)

    n_conv = conv_dim // cbw
    n_z = vd // cbw
    for cb in range(n_conv + 1):
        if cb < n_conv:
            cs_ = slice(cb * cbw, (cb + 1) * cbw)
            yb = _mm(h, w_ref[:, cs_])
            buf = cb % 2
            for hh in range(cbw // LANES):
                hs = slice(cb * cbw + hh * LANES, cb * cbw + (hh + 1) * LANES)
                pre_s[buf, hh, 0:halo, :] = halo_s[:, hs]
                pre_s[buf, hh, halo:halo + tm, :] = yb[:, hh * LANES:(hh + 1) * LANES]
            halo_s[:, cs_] = yb[tm - halo:, :]
        if cb > 0 and cb % (n_conv // n_z) == 0:
            zb = cb // (n_conv // n_z) - 1
            zs = slice(zb * cbw, (zb + 1) * cbw)
            z_ref[:, zs] = _mm(h, w_ref[:, conv_dim + zb * cbw:conv_dim + (zb + 1) * cbw]).astype(BF16)
        if cb > 0:
            conv_block(cb - 1)


def _gdn_in(x, nw, w, layer, wba, wat, cw, prm, prmt, seq, kd, vd):
    n, d = x.shape
    tm = TOKEN_TILE
    conv_dim = 2 * kd + vd
    row = lambda width: pl.BlockSpec((tm, width), lambda i: (i, 0))
    return pl.pallas_call(
        functools.partial(_gdn_in_body, tiles_per_seq=seq // tm, kd=kd, vd=vd),
        grid=(n // tm,),
        in_specs=[row(d), pl.BlockSpec((1, d), lambda i: (0, 0)), _resident_of(w, (layer,)), _resident(wba.shape),
                  _resident(wat.shape), _resident_of(cw, (layer,)), _resident(prm.shape), _resident(prmt.shape)],
        out_specs=[row(kd), row(kd), row(vd), row(vd), row(LANES),
                   pl.BlockSpec((tm // GDN_CHUNK, 2 * GDN_K_HEADS, LANES), lambda i: (i, 0, 0))],
        out_shape=[jax.ShapeDtypeStruct((n, kd), BF16), jax.ShapeDtypeStruct((n, kd), BF16),
                   jax.ShapeDtypeStruct((n, vd), BF16), jax.ShapeDtypeStruct((n, vd), BF16),
                   jax.ShapeDtypeStruct((n, LANES), F32),
                   jax.ShapeDtypeStruct((n // GDN_CHUNK, 2 * GDN_K_HEADS, LANES), F32)],
        scratch_shapes=[pltpu.VMEM((2, GDN_COL_BLOCK // LANES, tm + 8, LANES), F32),
                        pltpu.VMEM((8, conv_dim), F32),
                        pltpu.VMEM((GDN_COL_BLOCK // LANES, tm, LANES), F32)],
        compiler_params=_cparams(1),
        name="gdn_in",
    )(x, nw, w, wba, wat, cw, prm, prmt)


def _gdn_core_body(q_ref, k_ref, v_ref, z_ref, gb_ref, gbt_ref, nw_ref, o_ref, st_s):
    cs = GDN_CHUNK
    nchunks = q_ref.shape[1] // cs
    npairs = GDN_K_HEADS
    gofs = GDN_V_HEADS

    @pl.when(pl.program_id(1) == 0)
    def _():
        st_s[...] = jnp.zeros(st_s.shape, F32)

    row = lax.broadcasted_iota(jnp.int32, (cs, LANES), 0)
    lane = lax.broadcasted_iota(jnp.int32, (cs, LANES), 1)
    colp = lane % cs
    left = lane < cs
    incl = row >= colp
    strict = row > colp
    eye2 = jnp.where(row == colp, 1.0, 0.0).astype(F32)
    nw = nw_ref[...]

    def blockdiag(x):
        return jnp.concatenate([jnp.where(left, x, 0.0), jnp.where(left, 0.0, x)], axis=0).astype(BF16)

    def side(x, s):
        return x[:, :cs] if s == 0 else pltpu.roll(x, cs, 1)[:, :cs]

    pairs = [(c, hk) for c in range(nchunks) for hk in range(npairs)]
    xs, ps, qkds = [], [], []
    for c, hk in pairs:
        rs = slice(c * cs, (c + 1) * cs)
        ks = slice(hk * LANES, (hk + 1) * LANES)
        gbc = gb_ref[0, rs, :]
        qb = q_ref[0, rs, ks]
        kb = k_ref[0, rs, ks]
        kq = _mm_nt(jnp.concatenate([qb, kb], axis=0), jnp.concatenate([kb, kb], axis=0))
        ja, jb = 2 * hk, 2 * hk + 1
        beta2 = jnp.where(left, gbc[:, ja:ja + 1], gbc[:, jb:jb + 1])
        gc2 = jnp.where(left, gbc[:, gofs + ja:gofs + ja + 1], gbc[:, gofs + jb:gofs + jb + 1])
        dec2 = jnp.where(incl, jnp.exp(jnp.where(incl, gc2 - gbt_ref[c, hk:hk + 1, :], 0.0)), 0.0)
        x = jnp.where(strict, -(beta2 * kq[cs:] * dec2), 0.0)
        xs.append(x)
        ps.append(eye2 + x)
        qkds.append(kq[:cs] * dec2)

    bds = [blockdiag(x) for x in xs]
    for _ in range(5):
        xs = [_mm(x.astype(BF16), bd) for x, bd in zip(xs, bds)]
        bds = [blockdiag(x) for x in xs]
        ps = [p + _mm(p.astype(BF16), bd) for p, bd in zip(ps, bds)]

    us, wqs, kdecs, cdecs, qkdh = {}, {}, {}, {}, {}
    for i, (c, hk) in enumerate(pairs):
        rs = slice(c * cs, (c + 1) * cs)
        ks = slice(hk * LANES, (hk + 1) * LANES)
        gbc = gb_ref[0, rs, :]
        kb = k_ref[0, rs, ks]
        qf = q_ref[0, rs, ks].astype(F32)
        kf = kb.astype(F32)
        brow = gbt_ref[c, npairs + hk:npairs + hk + 1, :]
        pu = ps[i] * brow
        pw = ps[i] * (brow * jnp.exp(gbt_ref[c, hk:hk + 1, :]))
        w2 = _mm(jnp.concatenate([side(pw, 0), side(pw, 1)], axis=0).astype(BF16), kb)
        for s in range(2):
            j = 2 * hk + s
            vs = slice(j * LANES, (j + 1) * LANES)
            gc = gbc[:, gofs + j:gofs + j + 1]
            eg = jnp.exp(gc)
            gl = gc[cs - 1:cs, :]
            us[c, j] = _mm(side(pu, s).astype(BF16), v_ref[0, rs, vs])
            wqs[c, j] = jnp.concatenate([w2[s * cs:(s + 1) * cs], qf * eg], axis=0).astype(BF16)
            kdecs[c, j] = (kf * jnp.exp(gl - gc)).astype(BF16)
            cdecs[c, j] = jnp.exp(gl)
            qkdh[c, j] = side(qkds[i], s).astype(BF16)

    for c in range(nchunks):
        rs = slice(c * cs, (c + 1) * cs)
        states = [st_s[j] for j in range(GDN_V_HEADS)]
        boths = [_mm(wqs[c, j], states[j].astype(BF16)) for j in range(GDN_V_HEADS)]
        for j in range(GDN_V_HEADS):
            vs = slice(j * LANES, (j + 1) * LANES)
            vb = (us[c, j] - boths[j][:cs]).astype(BF16)
            o = boths[j][cs:] + _mm(qkdh[c, j], vb)
            st_s[j] = states[j] * cdecs[c, j] + _mm_tn(kdecs[c, j], vb)
            on = o * lax.rsqrt(jnp.mean(o * o, axis=-1, keepdims=True) + RMS_EPS) * nw
            o_ref[0, rs, vs] = (on * _silu(z_ref[0, rs, vs].astype(F32))).astype(BF16)


def _gdn_core(q, k, v, z, gb, gbt, nw, batch, seq):
    n, kd = q.shape
    vd = v.shape[1]
    tb = GDN_TIME_BLOCK
    nt = seq // tb
    tok = lambda width: pl.BlockSpec((1, tb, width), lambda b, t: (b, t, 0))
    o = pl.pallas_call(
        _gdn_core_body,
        grid=(batch, nt),
        in_specs=[tok(kd), tok(kd), tok(vd), tok(vd), tok(LANES),
                  pl.BlockSpec((tb // GDN_CHUNK, 2 * GDN_K_HEADS, LANES), lambda b, t: (b * nt + t, 0, 0)),
                  pl.BlockSpec((1, LANES), lambda b, t: (0, 0))],
        out_specs=tok(vd),
        out_shape=jax.ShapeDtypeStruct((batch, seq, vd), BF16),
        scratch_shapes=[pltpu.VMEM((GDN_V_HEADS, LANES, LANES), F32)],
        compiler_params=_cparams(2),
        name="gdn_core",
    )(q.reshape(batch, seq, kd), k.reshape(batch, seq, kd), v.reshape(batch, seq, vd), z.reshape(batch, seq, vd),
      gb.reshape(batch, seq, LANES), gbt, nw)
    return o.reshape(n, vd)


def _proj_out_body(x_ref, a_ref, w_ref, o_ref):
    o_ref[...] = x_ref[...] + _mm(a_ref[...], w_ref[...])


def _proj_out(x, a, w, layer):
    n, d = x.shape
    tm = WIDE_TILE
    return pl.pallas_call(
        _proj_out_body,
        grid=(n // tm,),
        in_specs=[pl.BlockSpec((tm, d), lambda i: (i, 0)), pl.BlockSpec((tm, a.shape[1]), lambda i: (i, 0)),
                  _resident_of(w, (layer,))],
        out_specs=pl.BlockSpec((tm, d), lambda i: (i, 0)),
        out_shape=jax.ShapeDtypeStruct((n, d), F32),
        compiler_params=_cparams(1),
        name="gdn_out",
    )(x, a, w)


def _rope_tables(seq):
    inv_freq = 1.0 / (ROPE_THETA ** (jnp.arange(0, LANES, 2, dtype=F32) / LANES))
    ang = jnp.arange(seq, dtype=F32)[:, None] * inv_freq[None, :]
    cos, sin = jnp.cos(ang), jnp.sin(ang)
    return jnp.concatenate([cos, cos], axis=-1), jnp.concatenate([-sin, sin], axis=-1)


def _gate_layouts(w_in, a_log, dt_bias, conv_dim, vd):
    nvh = GDN_V_HEADS
    wb = w_in[:, conv_dim + vd:conv_dim + vd + nvh]
    wa = w_in[:, conv_dim + vd + nvh:]
    wba = jnp.pad(jnp.concatenate([wb, wa], axis=1), ((0, 0), (0, LANES - 2 * nvh))).astype(BF16)
    wat = jnp.concatenate([wa[:, 0::2], wa[:, 1::2], wb[:, 0::2], wb[:, 1::2]], axis=1).T.astype(BF16)
    pad = lambda p: jnp.pad(p.reshape(1, nvh), ((0, 0), (nvh, LANES - 2 * nvh)))
    prm = jnp.pad(jnp.concatenate([pad(a_log), pad(dt_bias)], axis=0), ((0, 6), (0, 0)))
    eo = lambda p: jnp.broadcast_to(jnp.concatenate([p[0::2], p[1::2]]).reshape(nvh, 1), (nvh, LANES))
    prmt = jnp.stack([eo(a_log), eo(dt_bias)])
    return wba, wat, prm, prmt


def kernel(x, norm_w, ffn_w_in, ffn_w_out, attn_w_in, attn_w_out, gdn_w_in, gdn_conv_w, gdn_a_log,
           gdn_dt_bias, gdn_norm_w, gdn_w_out, final_norm_w):
    batch, seq, d = x.shape
    depth = norm_w.shape[0]
    kd = GDN_K_HEADS * LANES
    vd = GDN_V_HEADS * LANES
    conv_dim = 2 * kd + vd
    gw = HEADS_PER_GROUP * LANES
    assert seq % TOKEN_TILE == 0 and seq % GDN_TIME_BLOCK == 0
    assert all(w // dl == SPAN and seq % (dl * SPAN) == 0 for w, dl in DIL_PAIRS)

    cosf, sinf = _rope_tables(seq)
    ffn_wi, ffn_wo = ffn_w_in.astype(BF16), ffn_w_out.astype(BF16)
    attn_wi = attn_w_in.astype(BF16)
    attn_wo = attn_w_out.astype(BF16).reshape(attn_w_out.shape[0], len(DIL_PAIRS), gw, d)
    gdn_wi, gdn_wo = gdn_w_in.astype(BF16), gdn_w_out.astype(BF16)
    xf = x.reshape(batch * seq, d)
    fw = final_norm_w.reshape(1, d)
    ia = ib = 0
    for i in range(depth):
        def ffn(xf, j, final):
            return _ffn(xf, norm_w[i, 2 * j].reshape(1, d), ffn_wi, ffn_wo, (i, j), fw, final)

        xf = ffn(xf, 0, False)
        nw = norm_w[i, 1].reshape(1, d)
        if i % 2 == 0:
            qkv = _attn_in(xf, nw, attn_wi, ia, cosf, sinf, seq)
            outs = [_attn_core(qkv, g, dl, batch, seq) for g, (_, dl) in enumerate(DIL_PAIRS)]
            xf = _attn_out(xf, [o for o, _ in outs], [l for _, l in outs], attn_wo, ia)
            ia += 1
        else:
            w_in = gdn_w_in[ib]
            wba, wat, prm, prmt = _gate_layouts(w_in, gdn_a_log[ib], gdn_dt_bias[ib], conv_dim, vd)
            q, k, v, z, gb, gbt = _gdn_in(xf, nw, gdn_wi, ib, wba, wat, gdn_conv_w, prm, prmt, seq, kd, vd)
            on = _gdn_core(q, k, v, z, gb, gbt, gdn_norm_w[ib].reshape(1, LANES), batch, seq)
            xf = _proj_out(xf, on, gdn_wo, ib)
            ib += 1
        xf = ffn(xf, 1, i == depth - 1)
    return xf.reshape(batch, seq, d)
```

```python
import functools

import jax
import jax.numpy as jnp
from jax import lax
from jax.experimental import pallas as pl
from jax.experimental.pallas import tpu as pltpu

F32, BF16 = jnp.float32, jnp.bfloat16

RMS_EPS = 1e-6
L2_EPS = 1e-6
LANES = 128
ROPE_THETA = 10000.0
DIL_PAIRS = ((128, 1), (512, 4), (2048, 16))
HEADS_PER_GROUP = 4
SPAN = 128
GDN_K_HEADS = 8
GDN_V_HEADS = 16
GDN_CHUNK = 64
GDN_CONV = 4
GDN_COL_BLOCK = 256
GDN_TIME_BLOCK = 128
TOKEN_TILE = 512
WIDE_TILE = 1024
FFN_COL_BLOCK = 256
ATTN_ROWS = 2048
VMEM_LIMIT_BYTES = 56 * 1024 * 1024
NEG_BIG = -1e30


def _cparams(n_axes):
    return pltpu.CompilerParams(dimension_semantics=("arbitrary",) * n_axes,
                                vmem_limit_bytes=VMEM_LIMIT_BYTES)


def _resident(shape):
    nd = len(shape)
    return pl.BlockSpec(shape, lambda *_: (0,) * nd, pipeline_mode=pl.Buffered(1))


def _resident_of(stacked, lead):
    tail = stacked.shape[len(lead):]
    return pl.BlockSpec((None,) * len(lead) + tail, lambda *_: tuple(lead) + (0,) * len(tail),
                        pipeline_mode=pl.Buffered(1))


def _mm(a, b):
    return jnp.dot(a, b, preferred_element_type=F32)


def _mm_nt(a, b):
    return lax.dot_general(a, b, (((1,), (1,)), ((), ())), preferred_element_type=F32)


def _mm_tn(a, b):
    return lax.dot_general(a, b, (((0,), (0,)), ((), ())), preferred_element_type=F32)


def _rms(x, w):
    ms = jnp.mean(x * x, axis=-1, keepdims=True)
    return x * lax.rsqrt(ms + RMS_EPS) * w


def _silu(x):
    return x * (1.0 / (1.0 + jnp.exp(-x)))


def _softplus(x):
    return jnp.maximum(x, 0.0) + jnp.log1p(jnp.exp(-jnp.abs(x)))


def _ffn_body(x_ref, nw_ref, wi_ref, wo_ref, fw_ref, o_ref, *, final):
    x = x_ref[...]
    h = _rms(x, nw_ref[...]).astype(BF16)
    f = wo_ref.shape[0]
    ff = None
    for c in range(f // FFN_COL_BLOCK):
        fs = slice(c * FFN_COL_BLOCK, (c + 1) * FFN_COL_BLOCK)
        us = slice(f + c * FFN_COL_BLOCK, f + (c + 1) * FFN_COL_BLOCK)
        a = (_silu(_mm(h, wi_ref[:, fs])) * _mm(h, wi_ref[:, us])).astype(BF16)
        part = _mm(a, wo_ref[fs, :])
        ff = part if ff is None else ff + part
    y = x + 0.5 * ff
    if final:
        y = _rms(y, fw_ref[...])
    o_ref[...] = y


def _ffn(x, nw, wi, wo, lead, fw, final):
    n, d = x.shape
    tm = WIDE_TILE
    row = pl.BlockSpec((tm, d), lambda i: (i, 0))
    vec = pl.BlockSpec((1, d), lambda i: (0, 0))
    return pl.pallas_call(
        functools.partial(_ffn_body, final=final),
        grid=(n // tm,),
        in_specs=[row, vec, _resident_of(wi, lead), _resident_of(wo, lead), vec],
        out_specs=row,
        out_shape=jax.ShapeDtypeStruct((n, d), F32),
        compiler_params=pltpu.CompilerParams(dimension_semantics=("arbitrary",), vmem_limit_bytes=VMEM_LIMIT_BYTES,
                                             allow_input_fusion=[False, False, True, True, False]),
        name="ffn",
    )(x, nw, wi, wo, fw)


def _attn_in_body(x_ref, nw_ref, w_ref, cos_ref, sin_ref, o_ref, *, n_heads, scale):
    h = _rms(x_ref[...], nw_ref[...]).astype(BF16)
    y = _mm(h, w_ref[...])
    cosf = cos_ref[...]
    sinf = sin_ref[...]
    for hd in range(3 * n_heads):
        blk = y[:, hd * LANES:(hd + 1) * LANES]
        if hd < 2 * n_heads:
            blk = blk * cosf + pltpu.roll(blk, LANES // 2, 1) * sinf
        if hd < n_heads:
            blk = blk * scale
        o_ref[hd] = blk


def _attn_in(x, nw, w, layer, cosf, sinf, seq):
    n, d = x.shape
    c = w.shape[2]
    tm = TOKEN_TILE
    tiles_per_seq = seq // tm
    return pl.pallas_call(
        functools.partial(_attn_in_body, n_heads=c // (3 * LANES), scale=LANES ** -0.5),
        grid=(n // tm,),
        in_specs=[pl.BlockSpec((tm, d), lambda i: (i, 0)),
                  pl.BlockSpec((1, d), lambda i: (0, 0)),
                  _resident_of(w, (layer,)),
                  pl.BlockSpec((tm, LANES), lambda i: (i % tiles_per_seq, 0)),
                  pl.BlockSpec((tm, LANES), lambda i: (i % tiles_per_seq, 0))],
        out_specs=pl.BlockSpec((c // LANES, tm, LANES), lambda i: (0, i, 0)),
        out_shape=jax.ShapeDtypeStruct((c // LANES, n, LANES), F32),
        compiler_params=_cparams(1),
        name="attn_in",
    )(x, nw, w, cosf, sinf)


def _attn_core_body(q_ref, k_ref, v_ref, o_ref, lse_ref, kp_s, vp_s, lse_s, tmp_s, *, dil, nsub):
    n = pl.program_id(1)
    hd = pl.program_id(2)
    win = SPAN * dil
    inner = 4 if dil > 4 else dil

    @pl.when(hd == 0)
    def _():
        lse_s[...] = jnp.zeros(lse_s.shape, F32)

    @pl.when(n == 0)
    def _():
        kp_s[hd] = jnp.zeros(kp_s.shape[1:], BF16)
        vp_s[hd] = jnp.zeros(vp_s.shape[1:], BF16)

    def rows(r, i):
        start = i * win + r
        return slice(start, start + SPAN) if dil == 1 else pl.ds(start, SPAN, stride=dil)

    def gather(t, ref, i):
        if dil <= inner:
            return [ref[0, 0, rows(r, i), :].astype(BF16) for r in range(dil)]
        outer = dil // inner
        for ri in range(inner):
            tmp_s[t, ri] = ref[0, 0, pl.ds(i * win + ri, win // inner, stride=inner), :]
        return [tmp_s[t, r % inner, pl.ds(r // inner, SPAN, stride=outer), :].astype(BF16) for r in range(dil)]

    qi = lax.broadcasted_iota(jnp.int32, (SPAN, SPAN), 0)
    kj = lax.broadcasted_iota(jnp.int32, (SPAN, SPAN), 1)
    mask_c = kj <= qi
    mask_p = kj >= qi
    mask_p0 = jnp.logical_and(mask_p, n > 0)
    lane = lax.broadcasted_iota(jnp.int32, (SPAN, LANES), 1)

    blocks, scs, sps, vcs, vps = [], [], [], [], []
    kprev = [kp_s[hd, r] for r in range(dil)]
    vprev = [vp_s[hd, r] for r in range(dil)]
    for i in range(nsub):
        qs, kcs, vws = gather(0, q_ref, i), gather(1, k_ref, i), gather(2, v_ref, i)
        for r in range(dil):
            blocks.append((r, i))
            scs.append(jnp.where(mask_c, _mm_nt(qs[r], kcs[r]), NEG_BIG))
            sps.append(jnp.where(mask_p0 if i == 0 else mask_p, _mm_nt(qs[r], kprev[r]), NEG_BIG))
            vcs.append(vws[r])
            vps.append(vprev[r])
        kprev, vprev = kcs, vws
    for r in range(dil):
        kp_s[hd, r] = kprev[r]
        vp_s[hd, r] = vprev[r]
    pcs, pps, dens, lses = [], [], [], []
    for sc, sp in zip(scs, sps):
        m = jnp.maximum(jnp.max(sc, axis=-1, keepdims=True), jnp.max(sp, axis=-1, keepdims=True))
        pc = jnp.exp(sc - m)
        pp = jnp.exp(sp - m)
        den = jnp.sum(pc, axis=-1, keepdims=True) + jnp.sum(pp, axis=-1, keepdims=True)
        pcs.append(pc.astype(BF16))
        pps.append(pp.astype(BF16))
        dens.append(den)
        lses.append(m + jnp.log(den))
    for b, (r, i) in enumerate(blocks):
        o_ref[0, 0, rows(r, i), :] = (_mm(pcs[b], vcs[b]) + _mm(pps[b], vps[b])) / dens[b]
        lse_s[b] = jnp.where(lane == hd, lses[b], lse_s[b])

    @pl.when(hd == pl.num_programs(2) - 1)
    def _():
        for b, (r, i) in enumerate(blocks):
            lse_ref[0, rows(r, i), :] = lse_s[b]


def _attn_core(qkv, group, dil, batch, seq):
    nh = qkv.shape[0] // 3
    hpg = HEADS_PER_GROUP
    nsub = max(1, ATTN_ROWS // (SPAN * dil))
    rows = nsub * SPAN * dil
    qkv4 = qkv.reshape(3 * nh, batch, seq, LANES)

    def spec(which):
        return pl.BlockSpec((1, 1, rows, LANES), lambda b, n, h: (which * nh + group * hpg + h, b, n, 0))

    o, lse = pl.pallas_call(
        functools.partial(_attn_core_body, dil=dil, nsub=nsub),
        grid=(batch, seq // rows, hpg),
        in_specs=[spec(0), spec(1), spec(2)],
        out_specs=[pl.BlockSpec((1, 1, rows, LANES), lambda b, n, h: (h, b, n, 0)),
                   pl.BlockSpec((1, rows, LANES), lambda b, n, h: (b, n, 0))],
        out_shape=[jax.ShapeDtypeStruct((hpg, batch, seq, LANES), F32),
                   jax.ShapeDtypeStruct((batch, seq, LANES), F32)],
        scratch_shapes=[pltpu.VMEM((hpg, dil, SPAN, LANES), BF16), pltpu.VMEM((hpg, dil, SPAN, LANES), BF16),
                        pltpu.VMEM((nsub * dil, SPAN, LANES), F32),
                        pltpu.VMEM((3, 4, SPAN * dil // 4, LANES), F32)],
        compiler_params=_cparams(3),
        name=f"attn_core_d{dil}",
    )(qkv4, qkv4, qkv4)
    return o.reshape(hpg, batch * seq, LANES), lse.reshape(batch * seq, LANES)


def _attn_out_body(x_ref, o0_ref, o1_ref, o2_ref, l0_ref, l1_ref, l2_ref, w_ref, out_ref):
    ls = [l0_ref[...], l1_ref[...], l2_ref[...]]
    m = jnp.maximum(jnp.maximum(ls[0], ls[1]), ls[2])
    es = [jnp.exp(l - m) for l in ls]
    inv = 1.0 / (es[0] + es[1] + es[2])
    acc = x_ref[...]
    for g, o_ref in enumerate((o0_ref, o1_ref, o2_ref)):
        alpha = es[g] * inv
        scaled = jnp.concatenate([o_ref[hg] * alpha[:, hg:hg + 1] for hg in range(HEADS_PER_GROUP)], axis=1)
        acc = acc + _mm(scaled.astype(BF16), w_ref[g])
    out_ref[...] = acc


def _attn_out(x, os_, lses, w, layer):
    n, d = x.shape
    tm = WIDE_TILE
    row = lambda width: pl.BlockSpec((tm, width), lambda i: (i, 0))
    heads = pl.BlockSpec((HEADS_PER_GROUP, tm, LANES), lambda i: (0, i, 0))
    return pl.pallas_call(
        _attn_out_body,
        grid=(n // tm,),
        in_specs=[row(d)] + [heads] * 3 + [row(LANES)] * 3 + [_resident_of(w, (layer,))],
        out_specs=row(d),
        out_shape=jax.ShapeDtypeStruct((n, d), F32),
        compiler_params=_cparams(1),
        name="attn_out",
    )(x, *os_, *lses, w)


def _segmented_cumsum(x, axis):
    pos = lax.broadcasted_iota(jnp.int32, x.shape, axis) % GDN_CHUNK
    s = 1
    while s < GDN_CHUNK:
        x = x + jnp.where(pos >= s, pltpu.roll(x, s, axis), 0.0)
        s *= 2
    return x


def _gdn_in_body(x_ref, nw_ref, w_ref, wba_ref, wat_ref, cw_ref, prm_ref, prmt_ref,
                 q_ref, k_ref, v_ref, z_ref, gb_ref, gbt_ref, pre_s, halo_s, stage_s, *, tiles_per_seq, kd, vd):
    i = pl.program_id(0)
    tm = x_ref.shape[0]
    halo = 8
    conv_dim = 2 * kd + vd
    cbw = GDN_COL_BLOCK
    h = _rms(x_ref[...], nw_ref[...]).astype(BF16)

    @pl.when(i % tiles_per_seq == 0)
    def _():
        halo_s[...] = jnp.zeros(halo_s.shape, F32)

    def conv_block(cb):
        buf = cb % 2
        hrows = tm // 2
        for hh in range(cbw // LANES):
            c0 = cb * cbw + hh * LANES
            sl = slice(c0, c0 + LANES)
            taps = [cw_ref[j:j + 1, sl] for j in range(GDN_CONV)]
            first = halo - (GDN_CONV - 1)
            wins = [pre_s[buf, hh, pl.ds(first + k, hrows, stride=2), :] for k in range(GDN_CONV + 1)]
            for par in range(2):
                acc = wins[par] * taps[0]
                for j in range(1, GDN_CONV):
                    acc = acc + wins[par + j] * taps[j]
                a = _silu(acc)
                if c0 < 2 * kd:
                    r = lax.rsqrt(jnp.sum(a * a, axis=-1, keepdims=True) + L2_EPS)
                    a = a * (r * (LANES ** -0.5) if c0 < kd else r)
                stage_s[hh, pl.ds(par, hrows, stride=2), :] = a
            a = stage_s[hh].astype(BF16)
            if c0 < kd:
                q_ref[:, sl] = a
            elif c0 < 2 * kd:
                k_ref[:, c0 - kd:c0 - kd + LANES] = a
            else:
                v_ref[:, c0 - 2 * kd:c0 - 2 * kd + LANES] = a

    nvh = GDN_V_HEADS
    ba = _mm(h, wba_ref[...])
    lane = lax.broadcasted_iota(jnp.int32, ba.shape, 1)
    beta = 1.0 / (1.0 + jnp.exp(-ba))
    g = -jnp.exp(prm_ref[0:1, :]) * _softplus(ba + prm_ref[1:2, :])
    gb_ref[...] = jnp.where(lane < nvh, beta, _segmented_cumsum(g, 0))
    at = _mm_nt(wat_ref[...], h)
    cpl = LANES // GDN_CHUNK
    for p in range(tm // LANES):
        blk = at[:, p * LANES:(p + 1) * LANES]
        cum = _segmented_cumsum(-jnp.exp(prmt_ref[0]) * _softplus(blk[:nvh] + prmt_ref[1]), 1)
        bet = 1.0 / (1.0 + jnp.exp(-blk[nvh:]))
        left = lax.broadcasted_iota(jnp.int32, (nvh // 2, LANES), 1) < GDN_CHUNK
        for t, val in enumerate((cum, bet)):
            ev, od = val[:nvh // 2], val[nvh // 2:]
            rs = slice(t * nvh // 2, (t + 1) * nvh // 2)
            gbt_ref[cpl * p, rs, :] = jnp.where(left, ev, pltpu.roll(od, GDN_CHUNK, 1))
            gbt_ref[cpl * p + 1, rs, :] = jnp.where(left, pltpu.roll(ev, GDN_CHUNK, 1), od)

    n_conv = conv_dim // cbw
    n_z = vd // cbw
    for cb in range(n_conv + 1):
        if cb < n_conv:
            cs_ = slice(cb * cbw, (cb + 1) * cbw)
            yb = _mm(h, w_ref[:, cs_])
            buf = cb % 2
            for hh in range(cbw // LANES):
                hs = slice(cb * cbw + hh * LANES, cb * cbw + (hh + 1) * LANES)
                pre_s[buf, hh, 0:halo, :] = halo_s[:, hs]
                pre_s[buf, hh, halo:halo + tm, :] = yb[:, hh * LANES:(hh + 1) * LANES]
            halo_s[:, cs_] = yb[tm - halo:, :]
        if cb > 0 and cb % (n_conv // n_z) == 0:
            zb = cb // (n_conv // n_z) - 1
            zs = slice(zb * cbw, (zb + 1) * cbw)
            z_ref[:, zs] = _mm(h, w_ref[:, conv_dim + zb * cbw:conv_dim + (zb + 1) * cbw]).astype(BF16)
        if cb > 0:
            conv_block(cb - 1)


def _gdn_in(x, nw, w, layer, wba, wat, cw, prm, prmt, seq, kd, vd):
    n, d = x.shape
    tm = TOKEN_TILE
    conv_dim = 2 * kd + vd
    row = lambda width: pl.BlockSpec((tm, width), lambda i: (i, 0))
    return pl.pallas_call(
        functools.partial(_gdn_in_body, tiles_per_seq=seq // tm, kd=kd, vd=vd),
        grid=(n // tm,),
        in_specs=[row(d), pl.BlockSpec((1, d), lambda i: (0, 0)), _resident_of(w, (layer,)), _resident(wba.shape),
                  _resident(wat.shape), _resident_of(cw, (layer,)), _resident(prm.shape), _resident(prmt.shape)],
        out_specs=[row(kd), row(kd), row(vd), row(vd), row(LANES),
                   pl.BlockSpec((tm // GDN_CHUNK, 2 * GDN_K_HEADS, LANES), lambda i: (i, 0, 0))],
        out_shape=[jax.ShapeDtypeStruct((n, kd), BF16), jax.ShapeDtypeStruct((n, kd), BF16),
                   jax.ShapeDtypeStruct((n, vd), BF16), jax.ShapeDtypeStruct((n, vd), BF16),
                   jax.ShapeDtypeStruct((n, LANES), F32),
                   jax.ShapeDtypeStruct((n // GDN_CHUNK, 2 * GDN_K_HEADS, LANES), F32)],
        scratch_shapes=[pltpu.VMEM((2, GDN_COL_BLOCK // LANES, tm + 8, LANES), F32),
                        pltpu.VMEM((8, conv_dim), F32),
                        pltpu.VMEM((GDN_COL_BLOCK // LANES, tm, LANES), F32)],
        compiler_params=_cparams(1),
        name="gdn_in",
    )(x, nw, w, wba, wat, cw, prm, prmt)


def _gdn_core_body(q_ref, k_ref, v_ref, z_ref, gb_ref, gbt_ref, nw_ref, o_ref, st_s):
    cs = GDN_CHUNK
    nchunks = q_ref.shape[1] // cs
    npairs = GDN_K_HEADS
    gofs = GDN_V_HEADS

    @pl.when(pl.program_id(1) == 0)
    def _():
        st_s[...] = jnp.zeros(st_s.shape, F32)

    row = lax.broadcasted_iota(jnp.int32, (cs, LANES), 0)
    lane = lax.broadcasted_iota(jnp.int32, (cs, LANES), 1)
    colp = lane % cs
    left = lane < cs
    incl = row >= colp
    strict = row > colp
    eye2 = jnp.where(row == colp, 1.0, 0.0).astype(F32)
    nw = nw_ref[...]

    def blockdiag(x):
        return jnp.concatenate([jnp.where(left, x, 0.0), jnp.where(left, 0.0, x)], axis=0).astype(BF16)

    def side(x, s):
        return x[:, :cs] if s == 0 else pltpu.roll(x, cs, 1)[:, :cs]

    pairs = [(c, hk) for c in range(nchunks) for hk in range(npairs)]
    xs, ps, qkds = [], [], []
    for c, hk in pairs:
        rs = slice(c * cs, (c + 1) * cs)
        ks = slice(hk * LANES, (hk + 1) * LANES)
        gbc = gb_ref[0, rs, :]
        qb = q_ref[0, rs, ks]
        kb = k_ref[0, rs, ks]
        kq = _mm_nt(jnp.concatenate([qb, kb], axis=0), jnp.concatenate([kb, kb], axis=0))
        ja, jb = 2 * hk, 2 * hk + 1
        beta2 = jnp.where(left, gbc[:, ja:ja + 1], gbc[:, jb:jb + 1])
        gc2 = jnp.where(left, gbc[:, gofs + ja:gofs + ja + 1], gbc[:, gofs + jb:gofs + jb + 1])
        dec2 = jnp.where(incl, jnp.exp(jnp.where(incl, gc2 - gbt_ref[c, hk:hk + 1, :], 0.0)), 0.0)
        x = jnp.where(strict, -(beta2 * kq[cs:] * dec2), 0.0)
        xs.append(x)
        ps.append(eye2 + x)
        qkds.append(kq[:cs] * dec2)

    bds = [blockdiag(x) for x in xs]
    for _ in range(5):
        xs = [_mm(x.astype(BF16), bd) for x, bd in zip(xs, bds)]
        bds = [blockdiag(x) for x in xs]
        ps = [p + _mm(p.astype(BF16), bd) for p, bd in zip(ps, bds)]

    us, wqs, kdecs, cdecs, qkdh = {}, {}, {}, {}, {}
    for i, (c, hk) in enumerate(pairs):
        rs = slice(c * cs, (c + 1) * cs)
        ks = slice(hk * LANES, (hk + 1) * LANES)
        gbc = gb_ref[0, rs, :]
        kb = k_ref[0, rs, ks]
        qf = q_ref[0, rs, ks].astype(F32)
        kf = kb.astype(F32)
        brow = gbt_ref[c, npairs + hk:npairs + hk + 1, :]
        pu = ps[i] * brow
        pw = ps[i] * (brow * jnp.exp(gbt_ref[c, hk:hk + 1, :]))
        w2 = _mm(jnp.concatenate([side(pw, 0), side(pw, 1)], axis=0).astype(BF16), kb)
        for s in range(2):
            j = 2 * hk + s
            vs = slice(j * LANES, (j + 1) * LANES)
            gc = gbc[:, gofs + j:gofs + j + 1]
            eg = jnp.exp(gc)
            gl = gc[cs - 1:cs, :]
            us[c, j] = _mm(side(pu, s).astype(BF16), v_ref[0, rs, vs])
            wqs[c, j] = jnp.concatenate([w2[s * cs:(s + 1) * cs], qf * eg], axis=0).astype(BF16)
            kdecs[c, j] = (kf * jnp.exp(gl - gc)).astype(BF16)
            cdecs[c, j] = jnp.exp(gl)
            qkdh[c, j] = side(qkds[i], s).astype(BF16)

    for c in range(nchunks):
        rs = slice(c * cs, (c + 1) * cs)
        states = [st_s[j] for j in range(GDN_V_HEADS)]
        boths = [_mm(wqs[c, j], states[j].astype(BF16)) for j in range(GDN_V_HEADS)]
        for j in range(GDN_V_HEADS):
            vs = slice(j * LANES, (j + 1) * LANES)
            vb = (us[c, j] - boths[j][:cs]).astype(BF16)
            o = boths[j][cs:] + _mm(qkdh[c, j], vb)
            st_s[j] = states[j] * cdecs[c, j] + _mm_tn(kdecs[c, j], vb)
            on = o * lax.rsqrt(jnp.mean(o * o, axis=-1, keepdims=True) + RMS_EPS) * nw
            o_ref[0, rs, vs] = (on * _silu(z_ref[0, rs, vs].astype(F32))).astype(BF16)


def _gdn_core(q, k, v, z, gb, gbt, nw, batch, seq):
    n, kd = q.shape
    vd = v.shape[1]
    tb = GDN_TIME_BLOCK
    nt = seq // tb
    tok = lambda width: pl.BlockSpec((1, tb, width), lambda b, t: (b, t, 0))
    o = pl.pallas_call(
        _gdn_core_body,
        grid=(batch, nt),
        in_specs=[tok(kd), tok(kd), tok(vd), tok(vd), tok(LANES),
                  pl.BlockSpec((tb // GDN_CHUNK, 2 * GDN_K_HEADS, LANES), lambda b, t: (b * nt + t, 0, 0)),
                  pl.BlockSpec((1, LANES), lambda b, t: (0, 0))],
        out_specs=tok(vd),
        out_shape=jax.ShapeDtypeStruct((batch, seq, vd), BF16),
        scratch_shapes=[pltpu.VMEM((GDN_V_HEADS, LANES, LANES), F32)],
        compiler_params=_cparams(2),
        name="gdn_core",
    )(q.reshape(batch, seq, kd), k.reshape(batch, seq, kd), v.reshape(batch, seq, vd), z.reshape(batch, seq, vd),
      gb.reshape(batch, seq, LANES), gbt, nw)
    return o.reshape(n, vd)


def _proj_out_body(x_ref, a_ref, w_ref, o_ref):
    o_ref[...] = x_ref[...] + _mm(a_ref[...], w_ref[...])


def _proj_out(x, a, w, layer):
    n, d = x.shape
    tm = WIDE_TILE
    return pl.pallas_call(
        _proj_out_body,
        grid=(n // tm,),
        in_specs=[pl.BlockSpec((tm, d), lambda i: (i, 0)), pl.BlockSpec((tm, a.shape[1]), lambda i: (i, 0)),
                  _resident_of(w, (layer,))],
        out_specs=pl.BlockSpec((tm, d), lambda i: (i, 0)),
        out_shape=jax.ShapeDtypeStruct((n, d), F32),
        compiler_params=_cparams(1),
        name="gdn_out",
    )(x, a, w)


def _rope_tables(seq):
    inv_freq = 1.0 / (ROPE_THETA ** (jnp.arange(0, LANES, 2, dtype=F32) / LANES))
    ang = jnp.arange(seq, dtype=F32)[:, None] * inv_freq[None, :]
    cos, sin = jnp.cos(ang), jnp.sin(ang)
    return jnp.concatenate([cos, cos], axis=-1), jnp.concatenate([-sin, sin], axis=-1)


def _gate_layouts(w_in, a_log, dt_bias, conv_dim, vd):
    nvh = GDN_V_HEADS
    wb = w_in[:, conv_dim + vd:conv_dim + vd + nvh]
    wa = w_in[:, conv_dim + vd + nvh:]
    wba = jnp.pad(jnp.concatenate([wb, wa], axis=1), ((0, 0), (0, LANES - 2 * nvh))).astype(BF16)
    wat = jnp.concatenate([wa[:, 0::2], wa[:, 1::2], wb[:, 0::2], wb[:, 1::2]], axis=1).T.astype(BF16)
    pad = lambda p: jnp.pad(p.reshape(1, nvh), ((0, 0), (nvh, LANES - 2 * nvh)))
    prm = jnp.pad(jnp.concatenate([pad(a_log), pad(dt_bias)], axis=0), ((0, 6), (0, 0)))
    eo = lambda p: jnp.broadcast_to(jnp.concatenate([p[0::2], p[1::2]]).reshape(nvh, 1), (nvh, LANES))
    prmt = jnp.stack([eo(a_log), eo(dt_bias)])
    return wba, wat, prm, prmt


def kernel(x, norm_w, ffn_w_in, ffn_w_out, attn_w_in, attn_w_out, gdn_w_in, gdn_conv_w, gdn_a_log,
           gdn_dt_bias, gdn_norm_w, gdn_w_out, final_norm_w):
    batch, seq, d = x.shape
    depth = norm_w.shape[0]
    kd = GDN_K_HEADS * LANES
    vd = GDN_V_HEADS * LANES
    conv_dim = 2 * kd + vd
    gw = HEADS_PER_GROUP * LANES
    assert seq % TOKEN_TILE == 0 and seq % GDN_TIME_BLOCK == 0
    assert all(w // dl == SPAN and seq % (dl * SPAN) == 0 for w, dl in DIL_PAIRS)

    cosf, sinf = _rope_tables(seq)
    ffn_wi, ffn_wo = ffn_w_in.astype(BF16), ffn_w_out.astype(BF16)
    attn_wi = attn_w_in.astype(BF16)
    attn_wo = attn_w_out.astype(BF16).reshape(attn_w_out.shape[0], len(DIL_PAIRS), gw, d)
    gdn_wi, gdn_wo = gdn_w_in.astype(BF16), gdn_w_out.astype(BF16)
    xf = x.reshape(batch * seq, d)
    fw = final_norm_w.reshape(1, d)
    ia = ib = 0
    for i in range(depth):
        def ffn(xf, j, final):
            return _ffn(xf, norm_w[i, 2 * j].reshape(1, d), ffn_wi, ffn_wo, (i, j), fw, final)

        xf = ffn(xf, 0, False)
        nw = norm_w[i, 1].reshape(1, d)
        if i % 2 == 0:
            qkv = _attn_in(xf, nw, attn_wi, ia, cosf, sinf, seq)
            outs = [_attn_core(qkv, g, dl, batch, seq) for g, (_, dl) in enumerate(DIL_PAIRS)]
            xf = _attn_out(xf, [o for o, _ in outs], [l for _, l in outs], attn_wo, ia)
            ia += 1
        else:
            w_in = gdn_w_in[ib]
            wba, wat, prm, prmt = _gate_layouts(w_in, gdn_a_log[ib], gdn_dt_bias[ib], conv_dim, vd)
            q, k, v, z, gb, gbt = _gdn_in(xf, nw, gdn_wi, ib, wba, wat, gdn_conv_w, prm, prmt, seq, kd, vd)
            on = _gdn_core(q, k, v, z, gb, gbt, gdn_norm_w[ib].reshape(1, LANES), batch, seq)
            xf = _proj_out(xf, on, gdn_wo, ib)
            ib += 1
        xf = ffn(xf, 1, i == depth - 1)
    return xf.reshape(batch, seq, d)
```
